```python
import jax, jax.numpy as jnp
from jax import lax
import numpy as np

D_MODEL = 1024
BATCH = 2
SEQ = 8192
DEPTH = 2

N_MIXERS = 2
GLA_HEADS = 4
GLA_KEY_DIM = D_MODEL // 2
GLA_VAL_DIM = D_MODEL
GLA_HEAD_K = GLA_KEY_DIM // GLA_HEADS
GLA_HEAD_V = GLA_VAL_DIM // GLA_HEADS
GLA_GATE_RANK = 16
GLA_GATE_TAU = 16.0
GLA_CHUNK = 64
GLA_IN_DIM = 2 * GLA_KEY_DIM + 2 * GLA_VAL_DIM + GLA_GATE_RANK
SC_WIDTH = D_MODEL
CONV_WIDTH = 3
D_FF = ((8 * D_MODEL // 3 + 255) // 256) * 256
N_GLA = (DEPTH + 1) // 2
N_SC = DEPTH // 2
RMS_EPS = 1e-6

kernel_name = "hybrid_gla_shortconv_convffn"


def rmsnorm(x, g):
    x32 = x.astype(jnp.float32)
    y = x32 * lax.rsqrt(jnp.mean(x32 * x32, axis=-1, keepdims=True) + RMS_EPS)
    return y.astype(x.dtype) * g


def causal_dwconv(x, w):
    T = x.shape[1]
    K = w.shape[1]
    xp = jnp.pad(x, ((0, 0), (K - 1, 0), (0, 0)))
    y = xp[:, 0:T] * w[:, 0]
    for j in range(1, K):
        y = y + xp[:, j:j + T] * w[:, j]
    return y


def to_chunks(t, n_heads, head_dim):
    B, T, _ = t.shape
    n = T // GLA_CHUNK
    return t.reshape(B, n, GLA_CHUNK, n_heads, head_dim).transpose(0, 3, 1, 2, 4)


def gla_mixer(h, w_in, w_gate_up, b_gate, head_norm_g, w_out):
    B, T, _ = h.shape
    proj = h @ w_in
    s1 = GLA_KEY_DIM
    s2 = 2 * GLA_KEY_DIM
    s3 = s2 + GLA_VAL_DIM
    s4 = s3 + GLA_VAL_DIM
    q, k, v, g, a_low = proj[..., :s1], proj[..., s1:s2], proj[..., s2:s3], proj[..., s3:s4], proj[..., s4:]
    log_a = jax.nn.log_sigmoid((a_low @ w_gate_up + b_gate).astype(jnp.float32)) / GLA_GATE_TAU

    qc = to_chunks(q.astype(jnp.float32) * (GLA_HEAD_K ** -0.5), GLA_HEADS, GLA_HEAD_K)
    kc = to_chunks(k.astype(jnp.float32), GLA_HEADS, GLA_HEAD_K)
    vc = to_chunks(v.astype(jnp.float32), GLA_HEADS, GLA_HEAD_V)
    lc = to_chunks(log_a, GLA_HEADS, GLA_HEAD_K)

    bcum = jnp.cumsum(lc, axis=3)
    b_last = bcum[..., -1, :]
    q_in = qc * jnp.exp(bcum)
    k_in = kc * jnp.exp(-bcum)
    mask = jnp.tril(jnp.ones((GLA_CHUNK, GLA_CHUNK), dtype=bool))
    scores = jnp.where(mask, jnp.einsum('bhncd,bhnsd->bhncs', q_in, k_in), 0.0)
    o_intra = jnp.einsum('bhncs,bhnse->bhnce', scores, vc)
    k_dec = kc * jnp.exp(b_last[..., None, :] - bcum)
    chunk_kv = jnp.einsum('bhncd,bhnce->bhnde', k_dec, vc)

    def step(S, inp):
        q_n, kv_n, dec_n = inp
        o_n = jnp.einsum('bhcd,bhde->bhce', q_n, S)
        S = dec_n[..., None] * S + kv_n
        return S, o_n

    S0 = jnp.zeros((B, GLA_HEADS, GLA_HEAD_K, GLA_HEAD_V), jnp.float32)
    xs = (q_in.transpose(2, 0, 1, 3, 4), chunk_kv.transpose(2, 0, 1, 3, 4), jnp.exp(b_last).transpose(2, 0, 1, 3))
    _, o_inter = lax.scan(step, S0, xs)
    o = o_intra + o_inter.transpose(1, 2, 0, 3, 4)
    o = o.transpose(0, 2, 3, 1, 4).reshape(B, T, GLA_HEADS, GLA_HEAD_V)
    o = o * lax.rsqrt(jnp.mean(o * o, axis=-1, keepdims=True) + RMS_EPS)
    o = o.astype(h.dtype) * head_norm_g
    o = o.reshape(B, T, GLA_VAL_DIM) * jax.nn.silu(g)
    return o @ w_out


def short_conv_mixer(h, w_in, conv_w, w_out):
    proj = h @ w_in
    b, c, u = proj[..., :SC_WIDTH], proj[..., SC_WIDTH:2 * SC_WIDTH], proj[..., 2 * SC_WIDTH:]
    y = b * causal_dwconv(c * u, conv_w)
    return y @ w_out


def conv_ffn(h, w_up, conv_w, w_down):
    up = causal_dwconv(h @ w_up, conv_w)
    a, u = up[..., :D_FF], up[..., D_FF:]
    return (jax.nn.silu(a) * u) @ w_down


def setup_inputs(seed: int = 0) -> dict:
    key = jax.random.key(seed)
    ks = jax.random.split(key, 16)
    f32 = jnp.float32
    out_scale = (2.0 * DEPTH) ** -0.5
    nrm = lambda k, shape, s: jax.random.normal(k, shape, f32) * s
    return {
        "x": nrm(ks[0], (BATCH, SEQ, D_MODEL), 1.0),
        "norm_mix_g": 1.0 + nrm(ks[1], (DEPTH, D_MODEL), 0.02),
        "norm_ffn_g": 1.0 + nrm(ks[2], (DEPTH, D_MODEL), 0.02),
        "gla_w_in": nrm(ks[3], (N_GLA, D_MODEL, GLA_IN_DIM), D_MODEL ** -0.5),
        "gla_w_gate_up": nrm(ks[4], (N_GLA, GLA_GATE_RANK, GLA_KEY_DIM), GLA_GATE_RANK ** -0.5),
        "gla_b_gate": nrm(ks[5], (N_GLA, GLA_KEY_DIM), 0.1),
        "gla_head_norm_g": 1.0 + nrm(ks[6], (N_GLA, GLA_HEADS, GLA_HEAD_V), 0.02),
        "gla_w_out": nrm(ks[7], (N_GLA, GLA_VAL_DIM, D_MODEL), GLA_VAL_DIM ** -0.5 * out_scale),
        "sc_w_in": nrm(ks[8], (N_SC, D_MODEL, 3 * SC_WIDTH), D_MODEL ** -0.5),
        "sc_conv_w": nrm(ks[9], (N_SC, SC_WIDTH, CONV_WIDTH), CONV_WIDTH ** -0.5),
        "sc_w_out": nrm(ks[10], (N_SC, SC_WIDTH, D_MODEL), SC_WIDTH ** -0.5 * out_scale),
        "ffn_w_up": nrm(ks[11], (DEPTH, D_MODEL, 2 * D_FF), D_MODEL ** -0.5),
        "ffn_conv_w": nrm(ks[12], (DEPTH, 2 * D_FF, CONV_WIDTH), CONV_WIDTH ** -0.5),
        "ffn_w_down": nrm(ks[13], (DEPTH, D_FF, D_MODEL), D_FF ** -0.5 * out_scale),
        "final_norm_g": 1.0 + nrm(ks[14], (D_MODEL,), 0.02),
    }


def reference(x, norm_mix_g, norm_ffn_g, gla_w_in, gla_w_gate_up, gla_b_gate, gla_head_norm_g, gla_w_out,
              sc_w_in, sc_conv_w, sc_w_out, ffn_w_up, ffn_conv_w, ffn_w_down, final_norm_g):
    for i in range(DEPTH):
        h = rmsnorm(x, norm_mix_g[i])
        j = i // N_MIXERS
        if i % N_MIXERS == 0:
            x = x + gla_mixer(h, gla_w_in[j], gla_w_gate_up[j], gla_b_gate[j], gla_head_norm_g[j], gla_w_out[j])
        else:
            x = x + short_conv_mixer(h, sc_w_in[j], sc_conv_w[j], sc_w_out[j])
        h = rmsnorm(x, norm_ffn_g[i])
        x = x + conv_ffn(h, ffn_w_up[i], ffn_conv_w[i], ffn_w_down[i])
    return rmsnorm(x, final_norm_g)
```

```python
import functools

import jax
import jax.numpy as jnp
from jax import lax
from jax.experimental import pallas as pl
from jax.experimental.pallas import tpu as pltpu

F32 = jnp.float32
BF16 = jnp.bfloat16

D_MODEL = 1024
GLA_HEADS = 4
GLA_KEY_DIM = D_MODEL // 2
GLA_VAL_DIM = D_MODEL
GLA_HEAD_K = GLA_KEY_DIM // GLA_HEADS
GLA_HEAD_V = GLA_VAL_DIM // GLA_HEADS
GLA_GATE_RANK = 16
GLA_GATE_TAU = 16.0
GLA_CHUNK = 64
SC_WIDTH = D_MODEL
D_FF = ((8 * D_MODEL // 3 + 255) // 256) * 256
RMS_EPS = 1e-6

SUBLANES = 8
LANES = 128
VMEM_LIMIT = 56 * 1024 * 1024

TM = 512
TB = 256
FC = 256


def _rmsnorm(x, g):
    return x * lax.rsqrt(jnp.mean(x * x, axis=-1, keepdims=True) + RMS_EPS) * g


def _shift_rows(prev, cur, k):
    ext = jnp.concatenate([prev, cur], axis=0)
    return pltpu.roll(ext, k, 0)[SUBLANES:]


def _causal_conv3(prev, cur, cw):
    return (cw[0:1] * _shift_rows(prev, cur, 2) + cw[1:2] * _shift_rows(prev, cur, 1)
            + cw[2:3] * cur)


def _resident(shape):
    nd = len(shape)
    return pl.BlockSpec(shape, lambda *_: (0,) * nd, pipeline_mode=pl.Buffered(1))


def _params(*sem):
    return pltpu.CompilerParams(dimension_semantics=sem, vmem_limit_bytes=VMEM_LIMIT)


def _ffn_kernel(x_ref, g_ref, wup_ref, cw_ref, wdn_ref, gf_ref, o_ref, prev_ref, *,
                tiles_per_seq, n_chunks, final_norm):
    x = x_ref[...]
    h = _rmsnorm(x, g_ref[...]).astype(BF16)

    @pl.when(pl.program_id(0) % tiles_per_seq == 0)
    def _():
        prev_ref[...] = jnp.zeros_like(prev_ref)

    o_ref[...] = x

    def chunk(j, carry):
        up = jnp.dot(h, wup_ref[j], preferred_element_type=F32)
        prev = prev_ref[j]
        prev_ref[j] = up[TM - SUBLANES:]
        c = _causal_conv3(prev, up, cw_ref[j])
        a, u = c[:, :FC], c[:, FC:]
        act = (a * jax.nn.sigmoid(a) * u).astype(BF16)
        o_ref[...] += jnp.dot(act, wdn_ref[j], preferred_element_type=F32)
        return carry

    lax.fori_loop(0, n_chunks, chunk, 0)

    if final_norm:
        o_ref[...] = _rmsnorm(o_ref[...], gf_ref[...])


def _conv_ffn(x, g, w_up, conv_w, w_down, g_final, final_norm, seq):
    m = x.shape[0]
    nc = D_FF // FC
    wup = w_up.reshape(D_MODEL, 2, nc, FC).transpose(2, 0, 1, 3).reshape(nc, D_MODEL, 2 * FC).astype(BF16)
    cw = conv_w.T.reshape(3, 2, nc, FC).transpose(2, 0, 1, 3).reshape(nc, 3, 2 * FC)
    wdn = w_down.reshape(nc, FC, D_MODEL).astype(BF16)
    kern = functools.partial(_ffn_kernel, tiles_per_seq=seq // TM, n_chunks=nc, final_norm=final_norm)
    return pl.pallas_call(
        kern,
        out_shape=jax.ShapeDtypeStruct((m, D_MODEL), F32),
        grid=(m // TM,),
        in_specs=[
            pl.BlockSpec((TM, D_MODEL), lambda i: (i, 0)),
            _resident((1, D_MODEL)),
            _resident((nc, D_MODEL, 2 * FC)),
            _resident((nc, 3, 2 * FC)),
            _resident((nc, FC, D_MODEL)),
            _resident((1, D_MODEL)),
        ],
        out_specs=pl.BlockSpec((TM, D_MODEL), lambda i: (i, 0)),
        scratch_shapes=[pltpu.VMEM((nc, SUBLANES, 2 * FC), F32)],
        compiler_params=_params("arbitrary"),
        name="conv_ffn",
    )(x, g.reshape(1, D_MODEL), wup, cw, wdn, g_final.reshape(1, D_MODEL))


def _sc_kernel(x_ref, g_ref, win_ref, cw_ref, wout_ref, o_ref, prev_ref, *, tiles_per_seq, n_chunks):
    x = x_ref[...]
    h = _rmsnorm(x, g_ref[...]).astype(BF16)

    @pl.when(pl.program_id(0) % tiles_per_seq == 0)
    def _():
        prev_ref[...] = jnp.zeros_like(prev_ref)

    o_ref[...] = x

    def chunk(j, carry):
        p = jnp.dot(h, win_ref[j], preferred_element_type=F32)
        b, cu = p[:, :FC], p[:, FC:2 * FC] * p[:, 2 * FC:]
        prev = prev_ref[j]
        prev_ref[j] = cu[TM - SUBLANES:]
        y = (b * _causal_conv3(prev, cu, cw_ref[j])).astype(BF16)
        o_ref[...] += jnp.dot(y, wout_ref[j], preferred_element_type=F32)
        return carry

    lax.fori_loop(0, n_chunks, chunk, 0)


def _short_conv(x, g, w_in, conv_w, w_out, seq):
    m = x.shape[0]
    nc = SC_WIDTH // FC
    win = w_in.reshape(D_MODEL, 3, nc, FC).transpose(2, 0, 1, 3).reshape(nc, D_MODEL, 3 * FC).astype(BF16)
    cw = conv_w.T.reshape(3, nc, FC).transpose(1, 0, 2)
    wout = w_out.reshape(nc, FC, D_MODEL).astype(BF16)
    kern = functools.partial(_sc_kernel, tiles_per_seq=seq // TM, n_chunks=nc)
    return pl.pallas_call(
        kern,
        out_shape=jax.ShapeDtypeStruct((m, D_MODEL), F32),
        grid=(m // TM,),
        in_specs=[
            pl.BlockSpec((TM, D_MODEL), lambda i: (i, 0)),
            _resident((1, D_MODEL)),
            _resident((nc, D_MODEL, 3 * FC)),
            _resident((nc, 3, FC)),
            _resident((nc, FC, D_MODEL)),
        ],
        out_specs=pl.BlockSpec((TM, D_MODEL), lambda i: (i, 0)),
        scratch_shapes=[pltpu.VMEM((nc, SUBLANES, FC), F32)],
        compiler_params=_params("arbitrary"),
        name="short_conv",
    )(x, g.reshape(1, D_MODEL), win, cw, wout)


def _gla_in_kernel(x_ref, g_ref, wq_ref, wk_ref, wv_ref, wg_ref, wa_ref, wgu_ref, bg_ref,
                   q_ref, k_ref, v_ref, og_ref, bc_ref):
    h = _rmsnorm(x_ref[...], g_ref[...]).astype(BF16)
    q_ref[...] = jnp.dot(h, wq_ref[...], preferred_element_type=F32) * (GLA_HEAD_K ** -0.5)
    k_ref[...] = jnp.dot(h, wk_ref[...], preferred_element_type=F32)
    v_ref[...] = jnp.dot(h, wv_ref[...], preferred_element_type=F32).astype(BF16)
    og_ref[...] = jnp.dot(h, wg_ref[...], preferred_element_type=F32)
    a_low = jnp.dot(h, wa_ref[...], preferred_element_type=F32).astype(BF16)
    z = jnp.dot(a_low, wgu_ref[...], preferred_element_type=F32) + bg_ref[...]
    log_a = -(jnp.maximum(-z, 0.0) + jnp.log1p(jnp.exp(-jnp.abs(z)))) / GLA_GATE_TAU
    r = lax.broadcasted_iota(jnp.int32, (GLA_CHUNK, GLA_CHUNK), 0)
    c = lax.broadcasted_iota(jnp.int32, (GLA_CHUNK, GLA_CHUNK), 1)
    tril = (c <= r).astype(F32)
    for n in range(TM // GLA_CHUNK):
        rows = slice(n * GLA_CHUNK, (n + 1) * GLA_CHUNK)
        bc_ref[rows, :] = jnp.dot(tril, log_a[rows], preferred_element_type=F32,
                                  precision=lax.Precision.HIGHEST)


def _gla_in(x, g, w_in, w_gate_up, b_gate):
    m = x.shape[0]
    s1, s2, s3, s4 = GLA_KEY_DIM, 2 * GLA_KEY_DIM, 2 * GLA_KEY_DIM + GLA_VAL_DIM, 2 * GLA_KEY_DIM + 2 * GLA_VAL_DIM
    wq, wk, wv, wg = (w_in[:, a:b].astype(BF16) for a, b in ((0, s1), (s1, s2), (s2, s3), (s3, s4)))
    wa = jnp.pad(w_in[:, s4:], ((0, 0), (0, LANES - GLA_GATE_RANK))).astype(BF16)
    wgu = jnp.pad(w_gate_up, ((0, LANES - GLA_GATE_RANK), (0, 0))).astype(BF16)
    tile = lambda n: pl.BlockSpec((TM, n), lambda i: (i, 0))
    return pl.pallas_call(
        _gla_in_kernel,
        out_shape=[
            jax.ShapeDtypeStruct((m, GLA_KEY_DIM), F32),
            jax.ShapeDtypeStruct((m, GLA_KEY_DIM), F32),
            jax.ShapeDtypeStruct((m, GLA_VAL_DIM), BF16),
            jax.ShapeDtypeStruct((m, GLA_VAL_DIM), F32),
            jax.ShapeDtypeStruct((m, GLA_KEY_DIM), F32),
        ],
        grid=(m // TM,),
        in_specs=[
            tile(D_MODEL),
            _resident((1, D_MODEL)),
            _resident((D_MODEL, GLA_KEY_DIM)),
            _resident((D_MODEL, GLA_KEY_DIM)),
            _resident((D_MODEL, GLA_VAL_DIM)),
            _resident((D_MODEL, GLA_VAL_DIM)),
            _resident((D_MODEL, LANES)),
            _resident((LANES, GLA_KEY_DIM)),
            _resident((1, GLA_KEY_DIM)),
        ],
        out_specs=[tile(GLA_KEY_DIM), tile(GLA_KEY_DIM), tile(GLA_VAL_DIM), tile(GLA_VAL_DIM), tile(GLA_KEY_DIM)],
        compiler_params=_params("arbitrary"),
        name="gla_in",
    )(x, g.reshape(1, D_MODEL), wq, wk, wv, wg, wa, wgu, b_gate.reshape(1, GLA_KEY_DIM))


def _gla_core_kernel(x_ref, q_ref, k_ref, v_ref, og_ref, bc_ref, hg_ref, wout_ref, o_ref, s_ref, y_ref):
    @pl.when(pl.program_id(1) == 0)
    def _():
        s_ref[...] = jnp.zeros_like(s_ref)

    r = lax.broadcasted_iota(jnp.int32, (GLA_CHUNK, GLA_CHUNK), 0)
    c = lax.broadcasted_iota(jnp.int32, (GLA_CHUNK, GLA_CHUNK), 1)
    causal = c <= r

    for n in range(TB // GLA_CHUNK):
        rows = slice(n * GLA_CHUNK, (n + 1) * GLA_CHUNK)
        bc = bc_ref[rows, :]
        b_last = bc[GLA_CHUNK - 1:GLA_CHUNK, :]
        q_in = q_ref[rows, :] * jnp.exp(bc)
        k = k_ref[rows, :]
        k_in = k * jnp.exp(-bc)
        k_dec = k * jnp.exp(b_last - bc)
        dec = jnp.broadcast_to(jnp.exp(b_last), (GLA_CHUNK, GLA_KEY_DIM))
        for hd in range(GLA_HEADS):
            ks = slice(hd * GLA_HEAD_K, (hd + 1) * GLA_HEAD_K)
            vs = slice(hd * GLA_HEAD_V, (hd + 1) * GLA_HEAD_V)
            qh = q_in[:, ks].astype(BF16)
            kh = k_in[:, ks].astype(BF16)
            vh = v_ref[rows, vs]
            scores = lax.dot_general(qh, kh, (((1,), (1,)), ((), ())), preferred_element_type=F32)
            scores = jnp.where(causal, scores, 0.0).astype(BF16)
            s_old = s_ref[hd]
            o = (jnp.dot(scores, vh, preferred_element_type=F32)
                 + jnp.dot(qh, s_old.astype(BF16), preferred_element_type=F32))
            kt = jnp.concatenate([k_dec[:, ks], dec[:, ks]], axis=0).T
            kv = jnp.dot(kt[:, :GLA_CHUNK].astype(BF16), vh, preferred_element_type=F32)
            s_ref[hd] = kt[:, GLA_CHUNK:GLA_CHUNK + 1] * s_old + kv
            on = o * lax.rsqrt(jnp.mean(o * o, axis=-1, keepdims=True) + RMS_EPS) * hg_ref[:, vs]
            gate = og_ref[rows, vs]
            y_ref[rows, vs] = (on * (gate * jax.nn.sigmoid(gate))).astype(BF16)

    o_ref[...] = x_ref[...] + jnp.dot(y_ref[...], wout_ref[...], preferred_element_type=F32)


def _gla_core(x, q, k, v, og, bc, head_g, w_out, batch):
    m = x.shape[0]
    tiles = m // batch // TB
    tile = lambda n: pl.BlockSpec((TB, n), lambda b, t: (b * tiles + t, 0))
    return pl.pallas_call(
        _gla_core_kernel,
        out_shape=jax.ShapeDtypeStruct((m, D_MODEL), F32),
        grid=(batch, tiles),
        in_specs=[
            tile(D_MODEL), tile(GLA_KEY_DIM), tile(GLA_KEY_DIM), tile(GLA_VAL_DIM), tile(GLA_VAL_DIM),
            tile(GLA_KEY_DIM),
            pl.BlockSpec((1, GLA_VAL_DIM), lambda b, t: (0, 0), pipeline_mode=pl.Buffered(1)),
            pl.BlockSpec((GLA_VAL_DIM, D_MODEL), lambda b, t: (0, 0), pipeline_mode=pl.Buffered(1)),
        ],
        out_specs=tile(D_MODEL),
        scratch_shapes=[
            pltpu.VMEM((GLA_HEADS, GLA_HEAD_K, GLA_HEAD_V), F32),
            pltpu.VMEM((TB, GLA_VAL_DIM), BF16),
        ],
        compiler_params=_params("arbitrary", "arbitrary"),
        name="gla_core",
    )(x, q, k, v, og, bc, head_g.reshape(1, GLA_VAL_DIM), w_out.astype(BF16))


def kernel(x, norm_mix_g, norm_ffn_g, gla_w_in, gla_w_gate_up, gla_b_gate, gla_head_norm_g, gla_w_out,
           sc_w_in, sc_conv_w, sc_w_out, ffn_w_up, ffn_conv_w, ffn_w_down, final_norm_g):
    batch, seq, d = x.shape
    assert d == D_MODEL and seq % TM == 0 and seq % TB == 0
    xf = x.reshape(batch * seq, d)

    q, k, v, og, bc = _gla_in(xf, norm_mix_g[0], gla_w_in[0], gla_w_gate_up[0], gla_b_gate[0])
    xf = _gla_core(xf, q, k, v, og, bc, gla_head_norm_g[0], gla_w_out[0], batch)
    xf = _conv_ffn(xf, norm_ffn_g[0], ffn_w_up[0], ffn_conv_w[0], ffn_w_down[0], final_norm_g, False, seq)
    xf = _short_conv(xf, norm_mix_g[1], sc_w_in[0], sc_conv_w[0], sc_w_out[0], seq)
    xf = _conv_ffn(xf, norm_ffn_g[1], ffn_w_up[1], ffn_conv_w[1], ffn_w_down[1], final_norm_g, True, seq)
    return xf.reshape(batch, seq, d)
```

```python
import functools

import jax
import jax.numpy as jnp
from jax import lax
from jax.experimental import pallas as pl
from jax.experimental.pallas import tpu as pltpu

F32 = jnp.float32
BF16 = jnp.bfloat16

D_MODEL = 1024
GLA_HEADS = 4
GLA_KEY_DIM = D_MODEL // 2
GLA_VAL_DIM = D_MODEL
GLA_HEAD_K = GLA_KEY_DIM // GLA_HEADS
GLA_HEAD_V = GLA_VAL_DIM // GLA_HEADS
GLA_GATE_RANK = 16
GLA_GATE_TAU = 16.0
GLA_CHUNK = 64
SC_WIDTH = D_MODEL
D_FF = ((8 * D_MODEL // 3 + 255) // 256) * 256
RMS_EPS = 1e-6

SUBLANES = 8
LANES = 128
VMEM_LIMIT = 56 * 1024 * 1024

TM = 512
TB = 256
FC = 256


def _rmsnorm(x, g):
    return x * lax.rsqrt(jnp.mean(x * x, axis=-1, keepdims=True) + RMS_EPS) * g


def _stage_rows(work_ref, slot, val):
    for b in range(val.shape[1] // LANES):
        work_ref[slot, b, SUBLANES:, :] = val[:, b * LANES:(b + 1) * LANES]


def _causal_conv3_staged(work_ref, slot, carry_ref, j, cw):
    rows = work_ref.shape[2] - SUBLANES
    outs = []
    for b in range(work_ref.shape[1]):
        work_ref[slot, b, :SUBLANES, :] = carry_ref[j, b]
        carry_ref[j, b] = work_ref[slot, b, rows:, :]
        lanes = slice(b * LANES, (b + 1) * LANES)
        outs.append(cw[0:1, lanes] * work_ref[slot, b, SUBLANES - 2:SUBLANES - 2 + rows, :]
                    + cw[1:2, lanes] * work_ref[slot, b, SUBLANES - 1:SUBLANES - 1 + rows, :]
                    + cw[2:3, lanes] * work_ref[slot, b, SUBLANES:, :])
    return jnp.concatenate(outs, axis=1)


def _resident(shape):
    nd = len(shape)
    return pl.BlockSpec(shape, lambda *_: (0,) * nd, pipeline_mode=pl.Buffered(1))


def _params(*sem):
    return pltpu.CompilerParams(dimension_semantics=sem, vmem_limit_bytes=VMEM_LIMIT)


def _ffn_kernel(x_ref, g_ref, wup_ref, cw_ref, wdn_ref, gf_ref, o_ref, carry_ref, work_ref, *,
                tiles_per_seq, n_chunks, final_norm):
    x = x_ref[...]
    h = _rmsnorm(x, g_ref[...]).astype(BF16)

    @pl.when(pl.program_id(0) % tiles_per_seq == 0)
    def _():
        carry_ref[...] = jnp.zeros_like(carry_ref)

    o_ref[...] = x

    def up_proj(j):
        _stage_rows(work_ref, j % 2, jnp.dot(h, wup_ref[j], preferred_element_type=F32))

    def down_proj(j):
        c = _causal_conv3_staged(work_ref, j % 2, carry_ref, j, cw_ref[j])
        a, u = c[:, :FC], c[:, FC:]
        act = (a * jax.nn.sigmoid(a) * u).astype(BF16)
        o_ref[...] += jnp.dot(act, wdn_ref[j], preferred_element_type=F32)

    up_proj(0)
    for j in range(n_chunks):
        if j + 1 < n_chunks:
            up_proj(j + 1)
        down_proj(j)

    if final_norm:
        o_ref[...] = _rmsnorm(o_ref[...], gf_ref[...])


def _conv_ffn(x, g, w_up, conv_w, w_down, g_final, final_norm, seq):
    m = x.shape[0]
    nc = D_FF // FC
    wup = w_up.reshape(D_MODEL, 2, nc, FC).transpose(2, 0, 1, 3).reshape(nc, D_MODEL, 2 * FC).astype(BF16)
    cw = conv_w.T.reshape(3, 2, nc, FC).transpose(2, 0, 1, 3).reshape(nc, 3, 2 * FC)
    wdn = w_down.reshape(nc, FC, D_MODEL).astype(BF16)
    kern = functools.partial(_ffn_kernel, tiles_per_seq=seq // TM, n_chunks=nc, final_norm=final_norm)
    return pl.pallas_call(
        kern,
        out_shape=jax.ShapeDtypeStruct((m, D_MODEL), F32),
        grid=(m // TM,),
        in_specs=[
            pl.BlockSpec((TM, D_MODEL), lambda i: (i, 0)),
            _resident((1, D_MODEL)),
            _resident((nc, D_MODEL, 2 * FC)),
            _resident((nc, 3, 2 * FC)),
            _resident((nc, FC, D_MODEL)),
            _resident((1, D_MODEL)),
        ],
        out_specs=pl.BlockSpec((TM, D_MODEL), lambda i: (i, 0)),
        scratch_shapes=[
            pltpu.VMEM((nc, 2 * FC // LANES, SUBLANES, LANES), F32),
            pltpu.VMEM((2, 2 * FC // LANES, SUBLANES + TM, LANES), F32),
        ],
        compiler_params=_params("arbitrary"),
        name="conv_ffn",
    )(x, g.reshape(1, D_MODEL), wup, cw, wdn, g_final.reshape(1, D_MODEL))


def _sc_kernel(x_ref, g_ref, win_ref, cw_ref, wout_ref, o_ref, carry_ref, work_ref, b_ref, *,
               tiles_per_seq, n_chunks):
    x = x_ref[...]
    h = _rmsnorm(x, g_ref[...]).astype(BF16)

    @pl.when(pl.program_id(0) % tiles_per_seq == 0)
    def _():
        carry_ref[...] = jnp.zeros_like(carry_ref)

    o_ref[...] = x

    def in_proj(j):
        p = jnp.dot(h, win_ref[j], preferred_element_type=F32)
        b_ref[j % 2] = p[:, :FC]
        _stage_rows(work_ref, j % 2, p[:, FC:2 * FC] * p[:, 2 * FC:])

    def out_proj(j):
        conv = _causal_conv3_staged(work_ref, j % 2, carry_ref, j, cw_ref[j])
        y = (b_ref[j % 2] * conv).astype(BF16)
        o_ref[...] += jnp.dot(y, wout_ref[j], preferred_element_type=F32)

    in_proj(0)
    for j in range(n_chunks):
        if j + 1 < n_chunks:
            in_proj(j + 1)
        out_proj(j)


def _short_conv(x, g, w_in, conv_w, w_out, seq):
    m = x.shape[0]
    nc = SC_WIDTH // FC
    win = w_in.reshape(D_MODEL, 3, nc, FC).transpose(2, 0, 1, 3).reshape(nc, D_MODEL, 3 * FC).astype(BF16)
    cw = conv_w.T.reshape(3, nc, FC).transpose(1, 0, 2)
    wout = w_out.reshape(nc, FC, D_MODEL).astype(BF16)
    kern = functools.partial(_sc_kernel, tiles_per_seq=seq // TM, n_chunks=nc)
    return pl.pallas_call(
        kern,
        out_shape=jax.ShapeDtypeStruct((m, D_MODEL), F32),
        grid=(m // TM,),
        in_specs=[
            pl.BlockSpec((TM, D_MODEL), lambda i: (i, 0)),
            _resident((1, D_MODEL)),
            _resident((nc, D_MODEL, 3 * FC)),
            _resident((nc, 3, FC)),
            _resident((nc, FC, D_MODEL)),
        ],
        out_specs=pl.BlockSpec((TM, D_MODEL), lambda i: (i, 0)),
        scratch_shapes=[
            pltpu.VMEM((nc, FC // LANES, SUBLANES, LANES), F32),
            pltpu.VMEM((2, FC // LANES, SUBLANES + TM, LANES), F32),
            pltpu.VMEM((2, TM, FC), F32),
        ],
        compiler_params=_params("arbitrary"),
        name="short_conv",
    )(x, g.reshape(1, D_MODEL), win, cw, wout)


def _gla_in_kernel(x_ref, g_ref, wq_ref, wk_ref, wv_ref, wg_ref, wa_ref, wgu_ref, bg_ref,
                   q_ref, k_ref, v_ref, og_ref, bc_ref):
    h = _rmsnorm(x_ref[...], g_ref[...]).astype(BF16)
    q_ref[...] = jnp.dot(h, wq_ref[...], preferred_element_type=F32) * (GLA_HEAD_K ** -0.5)
    k_ref[...] = jnp.dot(h, wk_ref[...], preferred_element_type=F32)
    v_ref[...] = jnp.dot(h, wv_ref[...], preferred_element_type=F32).astype(BF16)
    og_ref[...] = jnp.dot(h, wg_ref[...], preferred_element_type=F32)
    a_low = jnp.dot(h, wa_ref[...], preferred_element_type=F32).astype(BF16)
    z = jnp.dot(a_low, wgu_ref[...], preferred_element_type=F32) + bg_ref[...]
    log_a = -(jnp.maximum(-z, 0.0) + jnp.log1p(jnp.exp(-jnp.abs(z)))) / GLA_GATE_TAU
    r = lax.broadcasted_iota(jnp.int32, (GLA_CHUNK, GLA_CHUNK), 0)
    c = lax.broadcasted_iota(jnp.int32, (GLA_CHUNK, GLA_CHUNK), 1)
    tril = (c <= r).astype(F32)
    for n in range(TM // GLA_CHUNK):
        rows = slice(n * GLA_CHUNK, (n + 1) * GLA_CHUNK)
        bc_ref[rows, :] = jnp.dot(tril, log_a[rows], preferred_element_type=F32,
                                  precision=lax.Precision.HIGHEST)


def _gla_in(x, g, w_in, w_gate_up, b_gate):
    m = x.shape[0]
    s1, s2, s3, s4 = GLA_KEY_DIM, 2 * GLA_KEY_DIM, 2 * GLA_KEY_DIM + GLA_VAL_DIM, 2 * GLA_KEY_DIM + 2 * GLA_VAL_DIM
    wq, wk, wv, wg = (w_in[:, a:b].astype(BF16) for a, b in ((0, s1), (s1, s2), (s2, s3), (s3, s4)))
    wa = jnp.pad(w_in[:, s4:], ((0, 0), (0, LANES - GLA_GATE_RANK))).astype(BF16)
    wgu = jnp.pad(w_gate_up, ((0, LANES - GLA_GATE_RANK), (0, 0))).astype(BF16)
    tile = lambda n: pl.BlockSpec((TM, n), lambda i: (i, 0))
    return pl.pallas_call(
        _gla_in_kernel,
        out_shape=[
            jax.ShapeDtypeStruct((m, GLA_KEY_DIM), F32),
            jax.ShapeDtypeStruct((m, GLA_KEY_DIM), F32),
            jax.ShapeDtypeStruct((m, GLA_VAL_DIM), BF16),
            jax.ShapeDtypeStruct((m, GLA_VAL_DIM), F32),
            jax.ShapeDtypeStruct((m, GLA_KEY_DIM), F32),
        ],
        grid=(m // TM,),
        in_specs=[
            tile(D_MODEL),
            _resident((1, D_MODEL)),
            _resident((D_MODEL, GLA_KEY_DIM)),
            _resident((D_MODEL, GLA_KEY_DIM)),
            _resident((D_MODEL, GLA_VAL_DIM)),
            _resident((D_MODEL, GLA_VAL_DIM)),
            _resident((D_MODEL, LANES)),
            _resident((LANES, GLA_KEY_DIM)),
            _resident((1, GLA_KEY_DIM)),
        ],
        out_specs=[tile(GLA_KEY_DIM), tile(GLA_KEY_DIM), tile(GLA_VAL_DIM), tile(GLA_VAL_DIM), tile(GLA_KEY_DIM)],
        compiler_params=_params("arbitrary"),
        name="gla_in",
    )(x, g.reshape(1, D_MODEL), wq, wk, wv, wg, wa, wgu, b_gate.reshape(1, GLA_KEY_DIM))


def _gla_core_kernel(x_ref, q_ref, k_ref, v_ref, og_ref, bc_ref, hg_ref, wout_ref, o_ref, s_ref, y_ref):
    @pl.when(pl.program_id(1) == 0)
    def _():
        s_ref[...] = jnp.zeros_like(s_ref)

    r = lax.broadcasted_iota(jnp.int32, (GLA_CHUNK, GLA_CHUNK), 0)
    c = lax.broadcasted_iota(jnp.int32, (GLA_CHUNK, GLA_CHUNK), 1)
    causal = c <= r

    for n in range(TB // GLA_CHUNK):
        rows = slice(n * GLA_CHUNK, (n + 1) * GLA_CHUNK)
        bc = bc_ref[rows, :]
        b_last = bc[GLA_CHUNK - 1:GLA_CHUNK, :]
        q_in = q_ref[rows, :] * jnp.exp(bc)
        k = k_ref[rows, :]
        k_in = k * jnp.exp(-bc)
        k_dec = k * jnp.exp(b_last - bc)
        dec = jnp.broadcast_to(jnp.exp(b_last), (GLA_CHUNK, GLA_KEY_DIM))
        for hd in range(GLA_HEADS):
            ks = slice(hd * GLA_HEAD_K, (hd + 1) * GLA_HEAD_K)
            vs = slice(hd * GLA_HEAD_V, (hd + 1) * GLA_HEAD_V)
            qh = q_in[:, ks].astype(BF16)
            kh = k_in[:, ks].astype(BF16)
            vh = v_ref[rows, vs]
            scores = lax.dot_general(qh, kh, (((1,), (1,)), ((), ())), preferred_element_type=F32)
            scores = jnp.where(causal, scores, 0.0).astype(BF16)
            s_old = s_ref[hd]
            o = (jnp.dot(scores, vh, preferred_element_type=F32)
                 + jnp.dot(qh, s_old.astype(BF16), preferred_element_type=F32))
            kt = jnp.concatenate([k_dec[:, ks], dec[:, ks]], axis=0).T
            kv = jnp.dot(kt[:, :GLA_CHUNK].astype(BF16), vh, preferred_element_type=F32)
            s_ref[hd] = kt[:, GLA_CHUNK:GLA_CHUNK + 1] * s_old + kv
            on = o * lax.rsqrt(jnp.mean(o * o, axis=-1, keepdims=True) + RMS_EPS) * hg_ref[:, vs]
            gate = og_ref[rows, vs]
            y_ref[rows, vs] = (on * (gate * jax.nn.sigmoid(gate))).astype(BF16)

    o_ref[...] = x_ref[...] + jnp.dot(y_ref[...], wout_ref[...], preferred_element_type=F32)


def _gla_core(x, q, k, v, og, bc, head_g, w_out, batch):
    m = x.shape[0]
    tiles = m // batch // TB
    tile = lambda n: pl.BlockSpec((TB, n), lambda b, t: (b * tiles + t, 0))
    return pl.pallas_call(
        _gla_core_kernel,
        out_shape=jax.ShapeDtypeStruct((m, D_MODEL), F32),
        grid=(batch, tiles),
        in_specs=[
            tile(D_MODEL), tile(GLA_KEY_DIM), tile(GLA_KEY_DIM), tile(GLA_VAL_DIM), tile(GLA_VAL_DIM),
            tile(GLA_KEY_DIM),
            pl.BlockSpec((1, GLA_VAL_DIM), lambda b, t: (0, 0), pipeline_mode=pl.Buffered(1)),
            pl.BlockSpec((GLA_VAL_DIM, D_MODEL), lambda b, t: (0, 0), pipeline_mode=pl.Buffered(1)),
        ],
        out_specs=tile(D_MODEL),
        scratch_shapes=[
            pltpu.VMEM((GLA_HEADS, GLA_HEAD_K, GLA_HEAD_V), F32),
            pltpu.VMEM((TB, GLA_VAL_DIM), BF16),
        ],
        compiler_params=_params("arbitrary", "arbitrary"),
        name="gla_core",
    )(x, q, k, v, og, bc, head_g.reshape(1, GLA_VAL_DIM), w_out.astype(BF16))


def kernel(x, norm_mix_g, norm_ffn_g, gla_w_in, gla_w_gate_up, gla_b_gate, gla_head_norm_g, gla_w_out,
           sc_w_in, sc_conv_w, sc_w_out, ffn_w_up, ffn_conv_w, ffn_w_down, final_norm_g):
    batch, seq, d = x.shape
    assert d == D_MODEL and seq % TM == 0 and seq % TB == 0
    xf = x.reshape(batch * seq, d)

    q, k, v, og, bc = _gla_in(xf, norm_mix_g[0], gla_w_in[0], gla_w_gate_up[0], gla_b_gate[0])
    xf = _gla_core(xf, q, k, v, og, bc, gla_head_norm_g[0], gla_w_out[0], batch)
    xf = _conv_ffn(xf, norm_ffn_g[0], ffn_w_up[0], ffn_conv_w[0], ffn_w_down[0], final_norm_g, False, seq)
    xf = _short_conv(xf, norm_mix_g[1], sc_w_in[0], sc_conv_w[0], sc_w_out[0], seq)
    xf = _conv_ffn(xf, norm_ffn_g[1], ffn_w_up[1], ffn_conv_w[1], ffn_w_down[1], final_norm_g, True, seq)
    return xf.reshape(batch, seq, d)
```

```python
import functools

import jax
import jax.numpy as jnp
from jax import lax
from jax.experimental import pallas as pl
from jax.experimental.pallas import tpu as pltpu

F32 = jnp.float32
BF16 = jnp.bfloat16

D_MODEL = 1024
GLA_HEADS = 4
GLA_KEY_DIM = D_MODEL // 2
GLA_VAL_DIM = D_MODEL
GLA_HEAD_K = GLA_KEY_DIM // GLA_HEADS
GLA_HEAD_V = GLA_VAL_DIM // GLA_HEADS
GLA_GATE_RANK = 16
GLA_GATE_TAU = 16.0
GLA_CHUNK = 64
SC_WIDTH = D_MODEL
D_FF = ((8 * D_MODEL // 3 + 255) // 256) * 256
RMS_EPS = 1e-6

SUBLANES = 8
LANES = 128
VMEM_LIMIT = 56 * 1024 * 1024

TM = 512
TB = 256
FC = 256


def _rmsnorm(x, g):
    return x * lax.rsqrt(jnp.mean(x * x, axis=-1, keepdims=True) + RMS_EPS) * g


def _stage_rows(work_ref, slot, val, first=0):
    for b in range(val.shape[1] // LANES):
        work_ref[slot, first + b, SUBLANES:, :] = val[:, b * LANES:(b + 1) * LANES]


def _causal_conv3_staged(work_ref, slot, carry_ref, j, cw):
    rows = work_ref.shape[2] - SUBLANES
    outs = []
    for b in range(work_ref.shape[1]):
        work_ref[slot, b, :SUBLANES, :] = carry_ref[j, b]
        carry_ref[j, b] = work_ref[slot, b, rows:, :]
        lanes = slice(b * LANES, (b + 1) * LANES)
        outs.append(cw[0:1, lanes] * work_ref[slot, b, SUBLANES - 2:SUBLANES - 2 + rows, :]
                    + cw[1:2, lanes] * work_ref[slot, b, SUBLANES - 1:SUBLANES - 1 + rows, :]
                    + cw[2:3, lanes] * work_ref[slot, b, SUBLANES:, :])
    return jnp.concatenate(outs, axis=1)


def _resident(shape):
    nd = len(shape)
    return pl.BlockSpec(shape, lambda *_: (0,) * nd, pipeline_mode=pl.Buffered(1))


def _params(*sem):
    return pltpu.CompilerParams(dimension_semantics=sem, vmem_limit_bytes=VMEM_LIMIT)


def _ffn_kernel(x_ref, g_ref, wup_ref, cw_ref, wdn_ref, gf_ref, o_ref, carry_ref, work_ref, *,
                tiles_per_seq, n_chunks, final_norm):
    x = x_ref[...]
    h = _rmsnorm(x, g_ref[...]).astype(BF16)

    @pl.when(pl.program_id(0) % tiles_per_seq == 0)
    def _():
        carry_ref[...] = jnp.zeros_like(carry_ref)

    o_ref[...] = x

    def cols(j):
        return slice(j * FC, (j + 1) * FC), slice(D_FF + j * FC, D_FF + (j + 1) * FC)

    def up_proj(j):
        for part, cs in enumerate(cols(j)):
            _stage_rows(work_ref, j % 2, jnp.dot(h, wup_ref[:, cs], preferred_element_type=F32),
                        part * FC // LANES)

    def down_proj(j):
        cw = jnp.concatenate([cw_ref[:, cs] for cs in cols(j)], axis=1)
        c = _causal_conv3_staged(work_ref, j % 2, carry_ref, j, cw)
        a, u = c[:, :FC], c[:, FC:]
        act = (a * jax.nn.sigmoid(a) * u).astype(BF16)
        o_ref[...] += jnp.dot(act, wdn_ref[j * FC:(j + 1) * FC, :], preferred_element_type=F32)

    up_proj(0)
    for j in range(n_chunks):
        if j + 1 < n_chunks:
            up_proj(j + 1)
        down_proj(j)

    if final_norm:
        o_ref[...] = _rmsnorm(o_ref[...], gf_ref[...])


def _conv_ffn(x, g, w_up, conv_w, w_down, g_final, final_norm, seq):
    m = x.shape[0]
    nc = D_FF // FC
    kern = functools.partial(_ffn_kernel, tiles_per_seq=seq // TM, n_chunks=nc, final_norm=final_norm)
    return pl.pallas_call(
        kern,
        out_shape=jax.ShapeDtypeStruct((m, D_MODEL), F32),
        grid=(m // TM,),
        in_specs=[
            pl.BlockSpec((TM, D_MODEL), lambda i: (i, 0)),
            _resident((1, D_MODEL)),
            _resident((D_MODEL, 2 * D_FF)),
            _resident((3, 2 * D_FF)),
            _resident((D_FF, D_MODEL)),
            _resident((1, D_MODEL)),
        ],
        out_specs=pl.BlockSpec((TM, D_MODEL), lambda i: (i, 0)),
        scratch_shapes=[
            pltpu.VMEM((nc, 2 * FC // LANES, SUBLANES, LANES), F32),
            pltpu.VMEM((2, 2 * FC // LANES, SUBLANES + TM, LANES), F32),
        ],
        compiler_params=_params("arbitrary"),
        name="conv_ffn",
    )(x, g.reshape(1, D_MODEL), w_up.astype(BF16), conv_w.T, w_down.astype(BF16), g_final.reshape(1, D_MODEL))


def _sc_kernel(x_ref, g_ref, win_ref, cw_ref, wout_ref, o_ref, carry_ref, work_ref, b_ref, *,
               tiles_per_seq, n_chunks):
    x = x_ref[...]
    h = _rmsnorm(x, g_ref[...]).astype(BF16)

    @pl.when(pl.program_id(0) % tiles_per_seq == 0)
    def _():
        carry_ref[...] = jnp.zeros_like(carry_ref)

    o_ref[...] = x

    def in_proj(j):
        b, c, u = (jnp.dot(h, win_ref[:, part * SC_WIDTH + j * FC:part * SC_WIDTH + (j + 1) * FC],
                           preferred_element_type=F32) for part in range(3))
        b_ref[j % 2] = b
        _stage_rows(work_ref, j % 2, c * u)

    def out_proj(j):
        conv = _causal_conv3_staged(work_ref, j % 2, carry_ref, j, cw_ref[:, j * FC:(j + 1) * FC])
        y = (b_ref[j % 2] * conv).astype(BF16)
        o_ref[...] += jnp.dot(y, wout_ref[j * FC:(j + 1) * FC, :], preferred_element_type=F32)

    in_proj(0)
    for j in range(n_chunks):
        if j + 1 < n_chunks:
            in_proj(j + 1)
        out_proj(j)


def _short_conv(x, g, w_in, conv_w, w_out, seq):
    m = x.shape[0]
    nc = SC_WIDTH // FC
    kern = functools.partial(_sc_kernel, tiles_per_seq=seq // TM, n_chunks=nc)
    return pl.pallas_call(
        kern,
        out_shape=jax.ShapeDtypeStruct((m, D_MODEL), F32),
        grid=(m // TM,),
        in_specs=[
            pl.BlockSpec((TM, D_MODEL), lambda i: (i, 0)),
            _resident((1, D_MODEL)),
            _resident((D_MODEL, 3 * SC_WIDTH)),
            _resident((3, SC_WIDTH)),
            _resident((SC_WIDTH, D_MODEL)),
        ],
        out_specs=pl.BlockSpec((TM, D_MODEL), lambda i: (i, 0)),
        scratch_shapes=[
            pltpu.VMEM((nc, FC // LANES, SUBLANES, LANES), F32),
            pltpu.VMEM((2, FC // LANES, SUBLANES + TM, LANES), F32),
            pltpu.VMEM((2, TM, FC), F32),
        ],
        compiler_params=_params("arbitrary"),
        name="short_conv",
    )(x, g.reshape(1, D_MODEL), w_in.astype(BF16), conv_w.T, w_out.astype(BF16))


def _chunk_cumsum(x):
    rows, w = x.shape
    per = GLA_CHUNK // SUBLANES
    g = x.reshape(rows // SUBLANES, SUBLANES, w)
    sub = lax.broadcasted_iota(jnp.int32, g.shape, 1)
    shift = 1
    while shift < SUBLANES:
        g = g + jnp.where(sub >= shift, pltpu.roll(g, shift, 1), 0.0)
        shift *= 2
    g = g.reshape(rows // GLA_CHUNK, per, SUBLANES, w)
    tot = jnp.broadcast_to(g[:, :, SUBLANES - 1:SUBLANES, :], g.shape)
    parts = [g[:, 0]]
    off = tot[:, 0]
    for i in range(1, per):
        parts.append(g[:, i] + off)
        off = off + tot[:, i]
    return jnp.stack(parts, axis=1).reshape(rows, w)


def _gla_in_kernel(x_ref, g_ref, wq_ref, wk_ref, wv_ref, wg_ref, wa_ref, wgu_ref, bg_ref,
                   q_ref, k_ref, v_ref, og_ref, bc_ref):
    h = _rmsnorm(x_ref[...], g_ref[...]).astype(BF16)
    a_low = jnp.dot(h, wa_ref[...], preferred_element_type=F32).astype(BF16)

    def decay(part):
        rows = slice(part * TM // 4, (part + 1) * TM // 4)
        z = jnp.dot(a_low[rows], wgu_ref[...], preferred_element_type=F32) + bg_ref[...]
        log_a = -(jnp.maximum(-z, 0.0) + jnp.log1p(jnp.exp(-jnp.abs(z)))) / GLA_GATE_TAU
        bc_ref[rows, :] = _chunk_cumsum(log_a)

    q_ref[...] = jnp.dot(h, wq_ref[...], preferred_element_type=F32) * (GLA_HEAD_K ** -0.5)
    decay(0)
    k_ref[...] = jnp.dot(h, wk_ref[...], preferred_element_type=F32)
    decay(1)
    half = GLA_VAL_DIM // 2
    v_ref[:, :half] = jnp.dot(h, wv_ref[:, :half], preferred_element_type=F32).astype(BF16)
    decay(2)
    v_ref[:, half:] = jnp.dot(h, wv_ref[:, half:], preferred_element_type=F32).astype(BF16)
    decay(3)
    og_ref[...] = jnp.dot(h, wg_ref[...], preferred_element_type=F32)


def _gla_in(x, g, w_in, w_gate_up, b_gate):
    m = x.shape[0]
    s1, s2, s3, s4 = GLA_KEY_DIM, 2 * GLA_KEY_DIM, 2 * GLA_KEY_DIM + GLA_VAL_DIM, 2 * GLA_KEY_DIM + 2 * GLA_VAL_DIM
    wq, wk, wv, wg = (w_in[:, a:b].astype(BF16) for a, b in ((0, s1), (s1, s2), (s2, s3), (s3, s4)))
    wa = jnp.pad(w_in[:, s4:], ((0, 0), (0, LANES - GLA_GATE_RANK))).astype(BF16)
    wgu = jnp.pad(w_gate_up, ((0, LANES - GLA_GATE_RANK), (0, 0))).astype(BF16)
    tile = lambda n: pl.BlockSpec((TM, n), lambda i: (i, 0))
    return pl.pallas_call(
        _gla_in_kernel,
        out_shape=[
            jax.ShapeDtypeStruct((m, GLA_KEY_DIM), F32),
            jax.ShapeDtypeStruct((m, GLA_KEY_DIM), F32),
            jax.ShapeDtypeStruct((m, GLA_VAL_DIM), BF16),
            jax.ShapeDtypeStruct((m, GLA_VAL_DIM), F32),
            jax.ShapeDtypeStruct((m, GLA_KEY_DIM), F32),
        ],
        grid=(m // TM,),
        in_specs=[
            tile(D_MODEL),
            _resident((1, D_MODEL)),
            _resident((D_MODEL, GLA_KEY_DIM)),
            _resident((D_MODEL, GLA_KEY_DIM)),
            _resident((D_MODEL, GLA_VAL_DIM)),
            _resident((D_MODEL, GLA_VAL_DIM)),
            _resident((D_MODEL, LANES)),
            _resident((LANES, GLA_KEY_DIM)),
            _resident((1, GLA_KEY_DIM)),
        ],
        out_specs=[tile(GLA_KEY_DIM), tile(GLA_KEY_DIM), tile(GLA_VAL_DIM), tile(GLA_VAL_DIM), tile(GLA_KEY_DIM)],
        compiler_params=_params("arbitrary"),
        name="gla_in",
    )(x, g.reshape(1, D_MODEL), wq, wk, wv, wg, wa, wgu, b_gate.reshape(1, GLA_KEY_DIM))


def _gla_core_kernel(x_ref, q_ref, k_ref, v_ref, og_ref, bc_ref, hg_ref, wout_ref, o_ref, s_ref, y_ref):
    @pl.when(pl.program_id(1) == 0)
    def _():
        s_ref[...] = jnp.zeros_like(s_ref)

    r = lax.broadcasted_iota(jnp.int32, (GLA_CHUNK, GLA_CHUNK), 0)
    c = lax.broadcasted_iota(jnp.int32, (GLA_CHUNK, GLA_CHUNK), 1)
    causal = c <= r

    for n in range(TB // GLA_CHUNK):
        rows = slice(n * GLA_CHUNK, (n + 1) * GLA_CHUNK)
        bc = bc_ref[rows, :]
        b_last = bc[GLA_CHUNK - 1:GLA_CHUNK, :]
        q_in = q_ref[rows, :] * jnp.exp(bc)
        k = k_ref[rows, :]
        k_in = k * jnp.exp(-bc)
        k_dec = k * jnp.exp(b_last - bc)
        dec = jnp.broadcast_to(jnp.exp(b_last), (GLA_CHUNK, GLA_KEY_DIM))
        for hd in range(GLA_HEADS):
            ks = slice(hd * GLA_HEAD_K, (hd + 1) * GLA_HEAD_K)
            vs = slice(hd * GLA_HEAD_V, (hd + 1) * GLA_HEAD_V)
            qh = q_in[:, ks].astype(BF16)
            kh = k_in[:, ks].astype(BF16)
            vh = v_ref[rows, vs]
            scores = lax.dot_general(qh, kh, (((1,), (1,)), ((), ())), preferred_element_type=F32)
            scores = jnp.where(causal, scores, 0.0).astype(BF16)
            s_old = s_ref[hd]
            o = (jnp.dot(scores, vh, preferred_element_type=F32)
                 + jnp.dot(qh, s_old.astype(BF16), preferred_element_type=F32))
            kt = jnp.concatenate([k_dec[:, ks], dec[:, ks]], axis=0).T
            kv = jnp.dot(kt[:, :GLA_CHUNK].astype(BF16), vh, preferred_element_type=F32)
            s_ref[hd] = kt[:, GLA_CHUNK:GLA_CHUNK + 1] * s_old + kv
            on = o * lax.rsqrt(jnp.mean(o * o, axis=-1, keepdims=True) + RMS_EPS) * hg_ref[:, vs]
            gate = og_ref[rows, vs]
            y_ref[rows, vs] = (on * (gate * jax.nn.sigmoid(gate))).astype(BF16)

    o_ref[...] = x_ref[...] + jnp.dot(y_ref[...], wout_ref[...], preferred_element_type=F32)


def _gla_core(x, q, k, v, og, bc, head_g, w_out, batch):
    m = x.shape[0]
    tiles = m // batch // TB
    tile = lambda n: pl.BlockSpec((TB, n), lambda b, t: (b * tiles + t, 0))
    return pl.pallas_call(
        _gla_core_kernel,
        out_shape=jax.ShapeDtypeStruct((m, D_MODEL), F32),
        grid=(batch, tiles),
        in_specs=[
            tile(D_MODEL), tile(GLA_KEY_DIM), tile(GLA_KEY_DIM), tile(GLA_VAL_DIM), tile(GLA_VAL_DIM),
            tile(GLA_KEY_DIM),
            pl.BlockSpec((1, GLA_VAL_DIM), lambda b, t: (0, 0), pipeline_mode=pl.Buffered(1)),
            pl.BlockSpec((GLA_VAL_DIM, D_MODEL), lambda b, t: (0, 0), pipeline_mode=pl.Buffered(1)),
        ],
        out_specs=tile(D_MODEL),
        scratch_shapes=[
            pltpu.VMEM((GLA_HEADS, GLA_HEAD_K, GLA_HEAD_V), F32),
            pltpu.VMEM((TB, GLA_VAL_DIM), BF16),
        ],
        compiler_params=_params("arbitrary", "arbitrary"),
        name="gla_core",
    )(x, q, k, v, og, bc, head_g.reshape(1, GLA_VAL_DIM), w_out.astype(BF16))


def kernel(x, norm_mix_g, norm_ffn_g, gla_w_in, gla_w_gate_up, gla_b_gate, gla_head_norm_g, gla_w_out,
           sc_w_in, sc_conv_w, sc_w_out, ffn_w_up, ffn_conv_w, ffn_w_down, final_norm_g):
    batch, seq, d = x.shape
    assert d == D_MODEL and seq % TM == 0 and seq % TB == 0
    xf = x.reshape(batch * seq, d)

    q, k, v, og, bc = _gla_in(xf, norm_mix_g[0], gla_w_in[0], gla_w_gate_up[0], gla_b_gate[0])
    xf = _gla_core(xf, q, k, v, og, bc, gla_head_norm_g[0], gla_w_out[0], batch)
    xf = _conv_ffn(xf, norm_ffn_g[0], ffn_w_up[0], ffn_conv_w[0], ffn_w_down[0], final_norm_g, False, seq)
    xf = _short_conv(xf, norm_mix_g[1], sc_w_in[0], sc_conv_w[0], sc_w_out[0], seq)
    xf = _conv_ffn(xf, norm_ffn_g[1], ffn_w_up[1], ffn_conv_w[1], ffn_w_down[1], final_norm_g, True, seq)
    return xf.reshape(batch, seq, d)
```

```python
import functools

import jax
import jax.numpy as jnp
from jax import lax
from jax.experimental import pallas as pl
from jax.experimental.pallas import tpu as pltpu

F32 = jnp.float32
BF16 = jnp.bfloat16

D_MODEL = 1024
GLA_HEADS = 4
GLA_KEY_DIM = D_MODEL // 2
GLA_VAL_DIM = D_MODEL
GLA_HEAD_K = GLA_KEY_DIM // GLA_HEADS
GLA_HEAD_V = GLA_VAL_DIM // GLA_HEADS
GLA_GATE_RANK = 16
GLA_GATE_TAU = 16.0
GLA_CHUNK = 64
SC_WIDTH = D_MODEL
D_FF = ((8 * D_MODEL // 3 + 255) // 256) * 256
RMS_EPS = 1e-6

SUBLANES = 8
LANES = 128
VMEM_LIMIT = 56 * 1024 * 1024

TM = 512
TB = 256
FC = 256


def _rmsnorm(x, g):
    return x * lax.rsqrt(jnp.mean(x * x, axis=-1, keepdims=True) + RMS_EPS) * g


def _stage_rows(work_ref, slot, val, first=0):
    for b in range(val.shape[1] // LANES):
        work_ref[slot, first + b, SUBLANES:, :] = val[:, b * LANES:(b + 1) * LANES]


def _causal_conv3_staged(work_ref, slot, carry_ref, j, cw):
    rows = work_ref.shape[2] - SUBLANES
    outs = []
    for b in range(work_ref.shape[1]):
        work_ref[slot, b, :SUBLANES, :] = carry_ref[j, b]
        carry_ref[j, b] = work_ref[slot, b, rows:, :]
        lanes = slice(b * LANES, (b + 1) * LANES)
        outs.append(cw[0:1, lanes] * work_ref[slot, b, SUBLANES - 2:SUBLANES - 2 + rows, :]
                    + cw[1:2, lanes] * work_ref[slot, b, SUBLANES - 1:SUBLANES - 1 + rows, :]
                    + cw[2:3, lanes] * work_ref[slot, b, SUBLANES:, :])
    return jnp.concatenate(outs, axis=1)


def _resident(shape):
    nd = len(shape)
    return pl.BlockSpec(shape, lambda *_: (0,) * nd, pipeline_mode=pl.Buffered(1))


def _resident_layer(shape, layer):
    nd = len(shape)
    return pl.BlockSpec((None,) + tuple(shape), lambda *_: (layer,) + (0,) * nd, pipeline_mode=pl.Buffered(1))


def _params(*sem):
    return pltpu.CompilerParams(dimension_semantics=sem, vmem_limit_bytes=VMEM_LIMIT)


def _ffn_kernel(x_ref, g_ref, wup_ref, cw_ref, wdn_ref, gf_ref, o_ref, carry_ref, work_ref, *,
                tiles_per_seq, n_chunks, final_norm):
    x = x_ref[...]
    h = _rmsnorm(x, g_ref[...]).astype(BF16)

    @pl.when(pl.program_id(0) % tiles_per_seq == 0)
    def _():
        carry_ref[...] = jnp.zeros_like(carry_ref)

    o_ref[...] = x

    def cols(j):
        return slice(j * FC, (j + 1) * FC), slice(D_FF + j * FC, D_FF + (j + 1) * FC)

    def up_proj(j):
        for part, cs in enumerate(cols(j)):
            _stage_rows(work_ref, j % 2, jnp.dot(h, wup_ref[:, cs].astype(BF16), preferred_element_type=F32),
                        part * FC // LANES)

    def down_proj(j):
        cw = jnp.concatenate([cw_ref[:, cs] for cs in cols(j)], axis=1)
        c = _causal_conv3_staged(work_ref, j % 2, carry_ref, j, cw)
        a, u = c[:, :FC], c[:, FC:]
        act = (a * jax.nn.sigmoid(a) * u).astype(BF16)
        o_ref[...] += jnp.dot(act, wdn_ref[j * FC:(j + 1) * FC, :].astype(BF16), preferred_element_type=F32)

    up_proj(0)
    for j in range(n_chunks):
        if j + 1 < n_chunks:
            up_proj(j + 1)
        down_proj(j)

    if final_norm:
        o_ref[...] = _rmsnorm(o_ref[...], gf_ref[...])


def _conv_ffn(x, g, w_up, conv_w, w_down, layer, g_final, final_norm, seq):
    m = x.shape[0]
    nc = D_FF // FC
    kern = functools.partial(_ffn_kernel, tiles_per_seq=seq // TM, n_chunks=nc, final_norm=final_norm)
    return pl.pallas_call(
        kern,
        out_shape=jax.ShapeDtypeStruct((m, D_MODEL), F32),
        grid=(m // TM,),
        in_specs=[
            pl.BlockSpec((TM, D_MODEL), lambda i: (i, 0)),
            _resident((1, D_MODEL)),
            _resident_layer((D_MODEL, 2 * D_FF), layer),
            _resident((3, 2 * D_FF)),
            _resident_layer((D_FF, D_MODEL), layer),
            _resident((1, D_MODEL)),
        ],
        out_specs=pl.BlockSpec((TM, D_MODEL), lambda i: (i, 0)),
        scratch_shapes=[
            pltpu.VMEM((nc, 2 * FC // LANES, SUBLANES, LANES), F32),
            pltpu.VMEM((2, 2 * FC // LANES, SUBLANES + TM, LANES), F32),
        ],
        compiler_params=_params("arbitrary"),
        name="conv_ffn",
    )(x, g.reshape(1, D_MODEL), w_up, conv_w.T, w_down, g_final.reshape(1, D_MODEL))


def _sc_kernel(x_ref, g_ref, win_ref, cw_ref, wout_ref, o_ref, carry_ref, work_ref, b_ref, *,
               tiles_per_seq, n_chunks):
    x = x_ref[...]
    h = _rmsnorm(x, g_ref[...]).astype(BF16)

    @pl.when(pl.program_id(0) % tiles_per_seq == 0)
    def _():
        carry_ref[...] = jnp.zeros_like(carry_ref)

    o_ref[...] = x

    def in_proj(j):
        b, c, u = (jnp.dot(h, win_ref[:, part * SC_WIDTH + j * FC:part * SC_WIDTH + (j + 1) * FC].astype(BF16),
                           preferred_element_type=F32) for part in range(3))
        b_ref[j % 2] = b
        _stage_rows(work_ref, j % 2, c * u)

    def out_proj(j):
        conv = _causal_conv3_staged(work_ref, j % 2, carry_ref, j, cw_ref[:, j * FC:(j + 1) * FC])
        y = (b_ref[j % 2] * conv).astype(BF16)
        o_ref[...] += jnp.dot(y, wout_ref[j * FC:(j + 1) * FC, :].astype(BF16), preferred_element_type=F32)

    in_proj(0)
    for j in range(n_chunks):
        if j + 1 < n_chunks:
            in_proj(j + 1)
        out_proj(j)


def _short_conv(x, g, w_in, conv_w, w_out, seq):
    m = x.shape[0]
    nc = SC_WIDTH // FC
    kern = functools.partial(_sc_kernel, tiles_per_seq=seq // TM, n_chunks=nc)
    return pl.pallas_call(
        kern,
        out_shape=jax.ShapeDtypeStruct((m, D_MODEL), F32),
        grid=(m // TM,),
        in_specs=[
            pl.BlockSpec((TM, D_MODEL), lambda i: (i, 0)),
            _resident((1, D_MODEL)),
            _resident((D_MODEL, 3 * SC_WIDTH)),
            _resident((3, SC_WIDTH)),
            _resident((SC_WIDTH, D_MODEL)),
        ],
        out_specs=pl.BlockSpec((TM, D_MODEL), lambda i: (i, 0)),
        scratch_shapes=[
            pltpu.VMEM((nc, FC // LANES, SUBLANES, LANES), F32),
            pltpu.VMEM((2, FC // LANES, SUBLANES + TM, LANES), F32),
            pltpu.VMEM((2, TM, FC), F32),
        ],
        compiler_params=_params("arbitrary"),
        name="short_conv",
    )(x, g.reshape(1, D_MODEL), w_in, conv_w.T, w_out)


def _chunk_cumsum(x):
    rows, w = x.shape
    per = GLA_CHUNK // SUBLANES
    g = x.reshape(rows // SUBLANES, SUBLANES, w)
    sub = lax.broadcasted_iota(jnp.int32, g.shape, 1)
    shift = 1
    while shift < SUBLANES:
        g = g + jnp.where(sub >= shift, pltpu.roll(g, shift, 1), 0.0)
        shift *= 2
    g = g.reshape(rows // GLA_CHUNK, per, SUBLANES, w)
    tot = jnp.broadcast_to(g[:, :, SUBLANES - 1:SUBLANES, :], g.shape)
    parts = [g[:, 0]]
    off = tot[:, 0]
    for i in range(1, per):
        parts.append(g[:, i] + off)
        off = off + tot[:, i]
    return jnp.stack(parts, axis=1).reshape(rows, w)


def _gla_in_kernel(x_ref, g_ref, win_ref, wa_ref, wgu_ref, bg_ref, q_ref, k_ref, v_ref, og_ref, bc_ref):
    h = _rmsnorm(x_ref[...], g_ref[...]).astype(BF16)

    def proj(start, width):
        return jnp.dot(h, win_ref[:, start:start + width].astype(BF16), preferred_element_type=F32)

    a_low = jnp.dot(h, wa_ref[...], preferred_element_type=F32).astype(BF16)

    def decay(part):
        rows = slice(part * TM // 4, (part + 1) * TM // 4)
        z = jnp.dot(a_low[rows], wgu_ref[...], preferred_element_type=F32) + bg_ref[...]
        log_a = -(jnp.maximum(-z, 0.0) + jnp.log1p(jnp.exp(-jnp.abs(z)))) / GLA_GATE_TAU
        bc_ref[rows, :] = _chunk_cumsum(log_a)

    half = GLA_VAL_DIM // 2
    q_ref[...] = proj(0, GLA_KEY_DIM) * (GLA_HEAD_K ** -0.5)
    decay(0)
    k_ref[...] = proj(GLA_KEY_DIM, GLA_KEY_DIM)
    decay(1)
    v_ref[:, :half] = proj(2 * GLA_KEY_DIM, half).astype(BF16)
    decay(2)
    v_ref[:, half:] = proj(2 * GLA_KEY_DIM + half, half).astype(BF16)
    decay(3)
    og_ref[...] = proj(2 * GLA_KEY_DIM + GLA_VAL_DIM, GLA_VAL_DIM)


def _gla_in(x, g, w_in, w_gate_up, b_gate):
    m = x.shape[0]
    wa = jnp.pad(w_in[:, 2 * GLA_KEY_DIM + 2 * GLA_VAL_DIM:], ((0, 0), (0, LANES - GLA_GATE_RANK))).astype(BF16)
    wgu = jnp.pad(w_gate_up, ((0, LANES - GLA_GATE_RANK), (0, 0))).astype(BF16)
    tile = lambda n: pl.BlockSpec((TM, n), lambda i: (i, 0))
    return pl.pallas_call(
        _gla_in_kernel,
        out_shape=[
            jax.ShapeDtypeStruct((m, GLA_KEY_DIM), F32),
            jax.ShapeDtypeStruct((m, GLA_KEY_DIM), F32),
            jax.ShapeDtypeStruct((m, GLA_VAL_DIM), BF16),
            jax.ShapeDtypeStruct((m, GLA_VAL_DIM), F32),
            jax.ShapeDtypeStruct((m, GLA_KEY_DIM), F32),
        ],
        grid=(m // TM,),
        in_specs=[
            tile(D_MODEL),
            _resident((1, D_MODEL)),
            _resident(w_in.shape),
            _resident((D_MODEL, LANES)),
            _resident((LANES, GLA_KEY_DIM)),
            _resident((1, GLA_KEY_DIM)),
        ],
        out_specs=[tile(GLA_KEY_DIM), tile(GLA_KEY_DIM), tile(GLA_VAL_DIM), tile(GLA_VAL_DIM), tile(GLA_KEY_DIM)],
        compiler_params=_params("arbitrary"),
        name="gla_in",
    )(x, g.reshape(1, D_MODEL), w_in, wa, wgu, b_gate.reshape(1, GLA_KEY_DIM))


def _gla_core_kernel(x_ref, q_ref, k_ref, v_ref, og_ref, bc_ref, hg_ref, wout_ref, o_ref, s_ref, y_ref):
    @pl.when(pl.program_id(1) == 0)
    def _():
        s_ref[...] = jnp.zeros_like(s_ref)

    r = lax.broadcasted_iota(jnp.int32, (GLA_CHUNK, GLA_CHUNK), 0)
    c = lax.broadcasted_iota(jnp.int32, (GLA_CHUNK, GLA_CHUNK), 1)
    causal = c <= r

    for n in range(TB // GLA_CHUNK):
        rows = slice(n * GLA_CHUNK, (n + 1) * GLA_CHUNK)
        bc = bc_ref[rows, :]
        b_last = bc[GLA_CHUNK - 1:GLA_CHUNK, :]
        q_in = q_ref[rows, :] * jnp.exp(bc)
        k = k_ref[rows, :]
        k_in = k * jnp.exp(-bc)
        k_dec = k * jnp.exp(b_last - bc)
        dec = jnp.broadcast_to(jnp.exp(b_last), (GLA_CHUNK, GLA_KEY_DIM))
        for hd in range(GLA_HEADS):
            ks = slice(hd * GLA_HEAD_K, (hd + 1) * GLA_HEAD_K)
            vs = slice(hd * GLA_HEAD_V, (hd + 1) * GLA_HEAD_V)
            qh = q_in[:, ks].astype(BF16)
            kh = k_in[:, ks].astype(BF16)
            vh = v_ref[rows, vs]
            scores = lax.dot_general(qh, kh, (((1,), (1,)), ((), ())), preferred_element_type=F32)
            scores = jnp.where(causal, scores, 0.0).astype(BF16)
            s_old = s_ref[hd]
            o = (jnp.dot(scores, vh, preferred_element_type=F32)
                 + jnp.dot(qh, s_old.astype(BF16), preferred_element_type=F32))
            kt = jnp.concatenate([k_dec[:, ks], dec[:, ks]], axis=0).T
            kv = jnp.dot(kt[:, :GLA_CHUNK].astype(BF16), vh, preferred_element_type=F32)
            s_ref[hd] = kt[:, GLA_CHUNK:GLA_CHUNK + 1] * s_old + kv
            on = o * lax.rsqrt(jnp.mean(o * o, axis=-1, keepdims=True) + RMS_EPS) * hg_ref[:, vs]
            gate = og_ref[rows, vs]
            y_ref[rows, vs] = (on * (gate * jax.nn.sigmoid(gate))).astype(BF16)

    o_ref[...] = x_ref[...] + jnp.dot(y_ref[...], wout_ref[...].astype(BF16), preferred_element_type=F32)


def _gla_core(x, q, k, v, og, bc, head_g, w_out, batch):
    m = x.shape[0]
    tiles = m // batch // TB
    tile = lambda n: pl.BlockSpec((TB, n), lambda b, t: (b * tiles + t, 0))
    return pl.pallas_call(
        _gla_core_kernel,
        out_shape=jax.ShapeDtypeStruct((m, D_MODEL), F32),
        grid=(batch, tiles),
        in_specs=[
            tile(D_MODEL), tile(GLA_KEY_DIM), tile(GLA_KEY_DIM), tile(GLA_VAL_DIM), tile(GLA_VAL_DIM),
            tile(GLA_KEY_DIM),
            pl.BlockSpec((1, GLA_VAL_DIM), lambda b, t: (0, 0), pipeline_mode=pl.Buffered(1)),
            pl.BlockSpec((GLA_VAL_DIM, D_MODEL), lambda b, t: (0, 0), pipeline_mode=pl.Buffered(1)),
        ],
        out_specs=tile(D_MODEL),
        scratch_shapes=[
            pltpu.VMEM((GLA_HEADS, GLA_HEAD_K, GLA_HEAD_V), F32),
            pltpu.VMEM((TB, GLA_VAL_DIM), BF16),
        ],
        compiler_params=_params("arbitrary", "arbitrary"),
        name="gla_core",
    )(x, q, k, v, og, bc, head_g.reshape(1, GLA_VAL_DIM), w_out)


def kernel(x, norm_mix_g, norm_ffn_g, gla_w_in, gla_w_gate_up, gla_b_gate, gla_head_norm_g, gla_w_out,
           sc_w_in, sc_conv_w, sc_w_out, ffn_w_up, ffn_conv_w, ffn_w_down, final_norm_g):
    batch, seq, d = x.shape
    assert d == D_MODEL and seq % TM == 0 and seq % TB == 0
    xf = x.reshape(batch * seq, d)

    q, k, v, og, bc = _gla_in(xf, norm_mix_g[0], gla_w_in[0], gla_w_gate_up[0], gla_b_gate[0])
    xf = _gla_core(xf, q, k, v, og, bc, gla_head_norm_g[0], gla_w_out[0], batch)
    xf = _conv_ffn(xf, norm_ffn_g[0], ffn_w_up, ffn_conv_w[0], ffn_w_down, 0, final_norm_g, False, seq)
    xf = _short_conv(xf, norm_mix_g[1], sc_w_in[0], sc_conv_w[0], sc_w_out[0], seq)
    xf = _conv_ffn(xf, norm_ffn_g[1], ffn_w_up, ffn_conv_w[1], ffn_w_down, 1, final_norm_g, True, seq)
    return xf.reshape(batch, seq, d)
```

```python
import functools

import jax
import jax.numpy as jnp
from jax import lax
from jax.experimental import pallas as pl
from jax.experimental.pallas import tpu as pltpu

F32 = jnp.float32
BF16 = jnp.bfloat16

D_MODEL = 1024
GLA_HEADS = 4
GLA_KEY_DIM = D_MODEL // 2
GLA_VAL_DIM = D_MODEL
GLA_HEAD_K = GLA_KEY_DIM // GLA_HEADS
GLA_HEAD_V = GLA_VAL_DIM // GLA_HEADS
GLA_GATE_RANK = 16
GLA_GATE_TAU = 16.0
GLA_CHUNK = 64
SC_WIDTH = D_MODEL
D_FF = ((8 * D_MODEL // 3 + 255) // 256) * 256
RMS_EPS = 1e-6

SUBLANES = 8
LANES = 128
VMEM_LIMIT = 56 * 1024 * 1024

TM = 512
TB = 256
FC = 256


def _rmsnorm(x, g):
    return x * lax.rsqrt(jnp.mean(x * x, axis=-1, keepdims=True) + RMS_EPS) * g


def _stage_rows(work_ref, slot, val, first=0):
    for b in range(val.shape[1] // LANES):
        work_ref[slot, first + b, SUBLANES:, :] = val[:, b * LANES:(b + 1) * LANES]


def _causal_conv3_staged(work_ref, slot, carry_ref, j, cw):
    rows = work_ref.shape[2] - SUBLANES
    outs = []
    for b in range(work_ref.shape[1]):
        work_ref[slot, b, :SUBLANES, :] = carry_ref[j, b]
        carry_ref[j, b] = work_ref[slot, b, rows:, :]
        lanes = slice(b * LANES, (b + 1) * LANES)
        outs.append(cw[0:1, lanes] * work_ref[slot, b, SUBLANES - 2:SUBLANES - 2 + rows, :]
                    + cw[1:2, lanes] * work_ref[slot, b, SUBLANES - 1:SUBLANES - 1 + rows, :]
                    + cw[2:3, lanes] * work_ref[slot, b, SUBLANES:, :])
    return jnp.concatenate(outs, axis=1)


def _resident(shape):
    nd = len(shape)
    return pl.BlockSpec(shape, lambda *_: (0,) * nd, pipeline_mode=pl.Buffered(1))


def _resident_layer(shape, layer):
    nd = len(shape)
    return pl.BlockSpec((None,) + tuple(shape), lambda *_: (layer,) + (0,) * nd, pipeline_mode=pl.Buffered(1))


def _params(*sem):
    return pltpu.CompilerParams(dimension_semantics=sem, vmem_limit_bytes=VMEM_LIMIT)


def _ffn_kernel(x_ref, g_ref, wup_ref, cw_ref, wdn_ref, gf_ref, o_ref, carry_ref, work_ref, *,
                tiles_per_seq, n_chunks, final_norm):
    x = x_ref[...]
    h = _rmsnorm(x, g_ref[...]).astype(BF16)

    @pl.when(pl.program_id(0) % tiles_per_seq == 0)
    def _():
        carry_ref[...] = jnp.zeros_like(carry_ref)

    o_ref[...] = x

    def cols(j):
        return slice(j * FC, (j + 1) * FC), slice(D_FF + j * FC, D_FF + (j + 1) * FC)

    def up_proj(j):
        for part, cs in enumerate(cols(j)):
            _stage_rows(work_ref, j % 2, jnp.dot(h, wup_ref[:, cs].astype(BF16), preferred_element_type=F32),
                        part * FC // LANES)

    def down_proj(j):
        cw = jnp.concatenate([cw_ref[:, cs] for cs in cols(j)], axis=1)
        c = _causal_conv3_staged(work_ref, j % 2, carry_ref, j, cw)
        a, u = c[:, :FC], c[:, FC:]
        act = (a * jax.nn.sigmoid(a) * u).astype(BF16)
        o_ref[...] += jnp.dot(act, wdn_ref[j * FC:(j + 1) * FC, :].astype(BF16), preferred_element_type=F32)

    up_proj(0)
    for j in range(n_chunks):
        if j + 1 < n_chunks:
            up_proj(j + 1)
        down_proj(j)

    if final_norm:
        o_ref[...] = _rmsnorm(o_ref[...], gf_ref[...])


def _conv_ffn(x, g, w_up, conv_w, w_down, layer, g_final, final_norm, seq):
    m = x.shape[0]
    nc = D_FF // FC
    kern = functools.partial(_ffn_kernel, tiles_per_seq=seq // TM, n_chunks=nc, final_norm=final_norm)
    return pl.pallas_call(
        kern,
        out_shape=jax.ShapeDtypeStruct((m, D_MODEL), F32),
        grid=(m // TM,),
        in_specs=[
            pl.BlockSpec((TM, D_MODEL), lambda i: (i, 0)),
            _resident((1, D_MODEL)),
            _resident_layer((D_MODEL, 2 * D_FF), layer),
            _resident((3, 2 * D_FF)),
            _resident_layer((D_FF, D_MODEL), layer),
            _resident((1, D_MODEL)),
        ],
        out_specs=pl.BlockSpec((TM, D_MODEL), lambda i: (i, 0)),
        scratch_shapes=[
            pltpu.VMEM((nc, 2 * FC // LANES, SUBLANES, LANES), F32),
            pltpu.VMEM((2, 2 * FC // LANES, SUBLANES + TM, LANES), F32),
        ],
        compiler_params=_params("arbitrary"),
        name="conv_ffn",
    )(x, g.reshape(1, D_MODEL), w_up, conv_w.T, w_down, g_final.reshape(1, D_MODEL))


def _sc_kernel(x_ref, g_ref, win_ref, cw_ref, wout_ref, o_ref, carry_ref, work_ref, b_ref, *,
               tiles_per_seq, n_chunks):
    x = x_ref[...]
    h = _rmsnorm(x, g_ref[...]).astype(BF16)

    @pl.when(pl.program_id(0) % tiles_per_seq == 0)
    def _():
        carry_ref[...] = jnp.zeros_like(carry_ref)

    o_ref[...] = x

    def in_proj(j):
        b, c, u = (jnp.dot(h, win_ref[:, part * SC_WIDTH + j * FC:part * SC_WIDTH + (j + 1) * FC].astype(BF16),
                           preferred_element_type=F32) for part in range(3))
        b_ref[j % 2] = b
        _stage_rows(work_ref, j % 2, c * u)

    def out_proj(j):
        conv = _causal_conv3_staged(work_ref, j % 2, carry_ref, j, cw_ref[:, j * FC:(j + 1) * FC])
        y = (b_ref[j % 2] * conv).astype(BF16)
        o_ref[...] += jnp.dot(y, wout_ref[j * FC:(j + 1) * FC, :].astype(BF16), preferred_element_type=F32)

    in_proj(0)
    for j in range(n_chunks):
        if j + 1 < n_chunks:
            in_proj(j + 1)
        out_proj(j)


def _short_conv(x, g, w_in, conv_w, w_out, seq):
    m = x.shape[0]
    nc = SC_WIDTH // FC
    kern = functools.partial(_sc_kernel, tiles_per_seq=seq // TM, n_chunks=nc)
    return pl.pallas_call(
        kern,
        out_shape=jax.ShapeDtypeStruct((m, D_MODEL), F32),
        grid=(m // TM,),
        in_specs=[
            pl.BlockSpec((TM, D_MODEL), lambda i: (i, 0)),
            _resident((1, D_MODEL)),
            _resident((D_MODEL, 3 * SC_WIDTH)),
            _resident((3, SC_WIDTH)),
            _resident((SC_WIDTH, D_MODEL)),
        ],
        out_specs=pl.BlockSpec((TM, D_MODEL), lambda i: (i, 0)),
        scratch_shapes=[
            pltpu.VMEM((nc, FC // LANES, SUBLANES, LANES), F32),
            pltpu.VMEM((2, FC // LANES, SUBLANES + TM, LANES), F32),
            pltpu.VMEM((2, TM, FC), F32),
        ],
        compiler_params=_params("arbitrary"),
        name="short_conv",
    )(x, g.reshape(1, D_MODEL), w_in, conv_w.T, w_out)


def _chunk_cumsum(x):
    rows, w = x.shape
    per = GLA_CHUNK // SUBLANES
    g = x.reshape(rows // SUBLANES, SUBLANES, w)
    sub = lax.broadcasted_iota(jnp.int32, g.shape, 1)
    shift = 1
    while shift < SUBLANES:
        g = g + jnp.where(sub >= shift, pltpu.roll(g, shift, 1), 0.0)
        shift *= 2
    g = g.reshape(rows // GLA_CHUNK, per, SUBLANES, w)
    tot = jnp.broadcast_to(g[:, :, SUBLANES - 1:SUBLANES, :], g.shape)
    parts = [g[:, 0]]
    off = tot[:, 0]
    for i in range(1, per):
        parts.append(g[:, i] + off)
        off = off + tot[:, i]
    return jnp.stack(parts, axis=1).reshape(rows, w)


def _gla_kernel(x_ref, g_ref, win_ref, wa_ref, wgu_ref, bg_ref, hg_ref, wout_ref, o_ref,
                st_ref, qin_ref, kin_ref, kdec_ref, v_ref, gate_ref, y_ref):
    n_chunks = TB // GLA_CHUNK
    nt = (((1,), (1,)), ((), ()))
    tn = (((0,), (0,)), ((), ()))

    @pl.when(pl.program_id(1) == 0)
    def _():
        st_ref[...] = jnp.zeros_like(st_ref)

    x = x_ref[...]
    h = _rmsnorm(x, g_ref[...]).astype(BF16)

    def proj(start, width):
        return jnp.dot(h, win_ref[:, start:start + width].astype(BF16), preferred_element_type=F32)

    a_low = jnp.dot(h, wa_ref[...], preferred_element_type=F32).astype(BF16)
    z = jnp.dot(a_low, wgu_ref[...], preferred_element_type=F32) + bg_ref[...]
    log_a = -(jnp.maximum(-z, 0.0) + jnp.log1p(jnp.exp(-jnp.abs(z)))) / GLA_GATE_TAU
    bcum = _chunk_cumsum(log_a).reshape(n_chunks, GLA_CHUNK, GLA_KEY_DIM)
    b_last = bcum[:, GLA_CHUNK - 1:, :]
    decay = jnp.exp(b_last)
    to_rows = lambda t: t.reshape(TB, GLA_KEY_DIM)

    q = proj(0, GLA_KEY_DIM) * (GLA_HEAD_K ** -0.5)
    qin_ref[...] = (q * to_rows(jnp.exp(bcum))).astype(BF16)
    k = proj(GLA_KEY_DIM, GLA_KEY_DIM)
    kin_ref[...] = (k * to_rows(jnp.exp(-bcum))).astype(BF16)
    kdec_ref[...] = (k * to_rows(jnp.exp(b_last - bcum))).astype(BF16)
    v_ref[...] = proj(2 * GLA_KEY_DIM, GLA_VAL_DIM).astype(BF16)
    gate_ref[...] = proj(2 * GLA_KEY_DIM + GLA_VAL_DIM, GLA_VAL_DIM)

    r = lax.broadcasted_iota(jnp.int32, (TB, TB), 0)
    c = lax.broadcasted_iota(jnp.int32, (TB, TB), 1)
    keep = (c <= r) & (c >= r - r % GLA_CHUNK)

    for hd in range(GLA_HEADS):
        ks = slice(hd * GLA_HEAD_K, (hd + 1) * GLA_HEAD_K)
        vs = slice(hd * GLA_HEAD_V, (hd + 1) * GLA_HEAD_V)
        qh, kh, vh = qin_ref[:, ks], kin_ref[:, ks], v_ref[:, vs]
        scores = lax.dot_general(qh, kh, nt, preferred_element_type=F32)
        o_intra = jnp.dot(jnp.where(keep, scores, 0.0).astype(BF16), vh, preferred_element_type=F32)
        st = st_ref[hd]
        o_inter = []
        for n in range(n_chunks):
            rows = slice(n * GLA_CHUNK, (n + 1) * GLA_CHUNK)
            o_inter.append(lax.dot_general(qh[rows], st.astype(BF16), nt, preferred_element_type=F32))
            kv_t = lax.dot_general(vh[rows], kdec_ref[rows, ks], tn, preferred_element_type=F32)
            st = decay[n, :, ks] * st + kv_t
        st_ref[hd] = st
        o = o_intra + jnp.concatenate(o_inter, axis=0)
        on = o * lax.rsqrt(jnp.mean(o * o, axis=-1, keepdims=True) + RMS_EPS) * hg_ref[:, vs]
        gate = gate_ref[:, vs]
        y_ref[:, vs] = (on * (gate * jax.nn.sigmoid(gate))).astype(BF16)

    o_ref[...] = x + jnp.dot(y_ref[...], wout_ref[...].astype(BF16), preferred_element_type=F32)


def _gla_mixer(x, g, w_in, w_gate_up, b_gate, head_g, w_out, batch):
    m = x.shape[0]
    tiles = m // batch // TB
    wa = jnp.pad(w_in[:, 2 * GLA_KEY_DIM + 2 * GLA_VAL_DIM:], ((0, 0), (0, LANES - GLA_GATE_RANK))).astype(BF16)
    wgu = jnp.pad(w_gate_up, ((0, LANES - GLA_GATE_RANK), (0, 0))).astype(BF16)
    tile = pl.BlockSpec((TB, D_MODEL), lambda b, t: (b * tiles + t, 0))
    return pl.pallas_call(
        _gla_kernel,
        out_shape=jax.ShapeDtypeStruct((m, D_MODEL), F32),
        grid=(batch, tiles),
        in_specs=[
            tile,
            _resident((1, D_MODEL)),
            _resident(w_in.shape),
            _resident((D_MODEL, LANES)),
            _resident((LANES, GLA_KEY_DIM)),
            _resident((1, GLA_KEY_DIM)),
            _resident((1, GLA_VAL_DIM)),
            _resident((GLA_VAL_DIM, D_MODEL)),
        ],
        out_specs=tile,
        scratch_shapes=[
            pltpu.VMEM((GLA_HEADS, GLA_HEAD_V, GLA_HEAD_K), F32),
            pltpu.VMEM((TB, GLA_KEY_DIM), BF16),
            pltpu.VMEM((TB, GLA_KEY_DIM), BF16),
            pltpu.VMEM((TB, GLA_KEY_DIM), BF16),
            pltpu.VMEM((TB, GLA_VAL_DIM), BF16),
            pltpu.VMEM((TB, GLA_VAL_DIM), F32),
            pltpu.VMEM((TB, GLA_VAL_DIM), BF16),
        ],
        compiler_params=_params("arbitrary", "arbitrary"),
        name="gla_mixer",
    )(x, g.reshape(1, D_MODEL), w_in, wa, wgu, b_gate.reshape(1, GLA_KEY_DIM),
      head_g.reshape(1, GLA_VAL_DIM), w_out)


def kernel(x, norm_mix_g, norm_ffn_g, gla_w_in, gla_w_gate_up, gla_b_gate, gla_head_norm_g, gla_w_out,
           sc_w_in, sc_conv_w, sc_w_out, ffn_w_up, ffn_conv_w, ffn_w_down, final_norm_g):
    batch, seq, d = x.shape
    assert d == D_MODEL and seq % TM == 0 and seq % TB == 0
    xf = x.reshape(batch * seq, d)

    xf = _gla_mixer(xf, norm_mix_g[0], gla_w_in[0], gla_w_gate_up[0], gla_b_gate[0],
                    gla_head_norm_g[0], gla_w_out[0], batch)
    xf = _conv_ffn(xf, norm_ffn_g[0], ffn_w_up, ffn_conv_w[0], ffn_w_down, 0, final_norm_g, False, seq)
    xf = _short_conv(xf, norm_mix_g[1], sc_w_in[0], sc_conv_w[0], sc_w_out[0], seq)
    xf = _conv_ffn(xf, norm_ffn_g[1], ffn_w_up, ffn_conv_w[1], ffn_w_down, 1, final_norm_g, True, seq)
    return xf.reshape(batch, seq, d)
```

```python
import functools

import jax
import jax.numpy as jnp
from jax import lax
from jax.experimental import pallas as pl
from jax.experimental.pallas import tpu as pltpu

F32 = jnp.float32
BF16 = jnp.bfloat16

D_MODEL = 1024
GLA_HEADS = 4
GLA_KEY_DIM = D_MODEL // 2
GLA_VAL_DIM = D_MODEL
GLA_HEAD_K = GLA_KEY_DIM // GLA_HEADS
GLA_HEAD_V = GLA_VAL_DIM // GLA_HEADS
GLA_GATE_RANK = 16
GLA_GATE_TAU = 16.0
GLA_CHUNK = 64
SC_WIDTH = D_MODEL
D_FF = ((8 * D_MODEL // 3 + 255) // 256) * 256
RMS_EPS = 1e-6

SUBLANES = 8
LANES = 128
VMEM_LIMIT = 56 * 1024 * 1024

TM = 512
TB = 256
FC = 256
FFN_AHEAD = 2
SC_AHEAD = 1


def _rmsnorm(x, g):
    return x * lax.rsqrt(jnp.mean(x * x, axis=-1, keepdims=True) + RMS_EPS) * g


def _stage_rows(work_ref, slot, val, first=0):
    for b in range(val.shape[1] // LANES):
        work_ref[slot, first + b, SUBLANES:, :] = val[:, b * LANES:(b + 1) * LANES]


def _causal_conv3_staged(work_ref, slot, carry_ref, j, cw):
    rows = work_ref.shape[2] - SUBLANES
    outs = []
    for b in range(work_ref.shape[1]):
        work_ref[slot, b, :SUBLANES, :] = carry_ref[j, b]
        carry_ref[j, b] = work_ref[slot, b, rows:, :]
        lanes = slice(b * LANES, (b + 1) * LANES)
        outs.append(cw[0:1, lanes] * work_ref[slot, b, SUBLANES - 2:SUBLANES - 2 + rows, :]
                    + cw[1:2, lanes] * work_ref[slot, b, SUBLANES - 1:SUBLANES - 1 + rows, :]
                    + cw[2:3, lanes] * work_ref[slot, b, SUBLANES:, :])
    return jnp.concatenate(outs, axis=1)


def _resident(shape):
    nd = len(shape)
    return pl.BlockSpec(shape, lambda *_: (0,) * nd, pipeline_mode=pl.Buffered(1))


def _resident_layer(shape, layer):
    nd = len(shape)
    return pl.BlockSpec((None,) + tuple(shape), lambda *_: (layer,) + (0,) * nd, pipeline_mode=pl.Buffered(1))


def _params(*sem):
    return pltpu.CompilerParams(dimension_semantics=sem, vmem_limit_bytes=VMEM_LIMIT)


def _ffn_kernel(x_ref, g_ref, wup_ref, cw_ref, wdn_ref, gf_ref, o_ref, carry_ref, work_ref, *,
                tiles_per_seq, n_chunks, final_norm):
    slots = work_ref.shape[0]
    h = _rmsnorm(x_ref[...], g_ref[...]).astype(BF16)

    @pl.when(pl.program_id(0) % tiles_per_seq == 0)
    def _():
        carry_ref[...] = jnp.zeros_like(carry_ref)

    def cols(j):
        return slice(j * FC, (j + 1) * FC), slice(D_FF + j * FC, D_FF + (j + 1) * FC)

    def up_proj(j):
        for part, cs in enumerate(cols(j)):
            _stage_rows(work_ref, j % slots, jnp.dot(h, wup_ref[:, cs].astype(BF16), preferred_element_type=F32),
                        part * FC // LANES)

    def down_proj(j):
        cw = jnp.concatenate([cw_ref[:, cs] for cs in cols(j)], axis=1)
        c = _causal_conv3_staged(work_ref, j % slots, carry_ref, j, cw)
        a, u = c[:, :FC], c[:, FC:]
        act = (a * jax.nn.sigmoid(a) * u).astype(BF16)
        base = x_ref if j == 0 else o_ref
        o_ref[...] = base[...] + jnp.dot(act, wdn_ref[j * FC:(j + 1) * FC, :].astype(BF16),
                                         preferred_element_type=F32)

    for j in range(slots - 1):
        up_proj(j)
    for j in range(n_chunks):
        if j + slots - 1 < n_chunks:
            up_proj(j + slots - 1)
        down_proj(j)

    if final_norm:
        o_ref[...] = _rmsnorm(o_ref[...], gf_ref[...])


def _conv_ffn(x, g, w_up, conv_w, w_down, layer, g_final, final_norm, seq):
    m = x.shape[0]
    nc = D_FF // FC
    kern = functools.partial(_ffn_kernel, tiles_per_seq=seq // TM, n_chunks=nc, final_norm=final_norm)
    return pl.pallas_call(
        kern,
        out_shape=jax.ShapeDtypeStruct((m, D_MODEL), F32),
        grid=(m // TM,),
        in_specs=[
            pl.BlockSpec((TM, D_MODEL), lambda i: (i, 0)),
            _resident((1, D_MODEL)),
            _resident_layer((D_MODEL, 2 * D_FF), layer),
            _resident((3, 2 * D_FF)),
            _resident_layer((D_FF, D_MODEL), layer),
            _resident((1, D_MODEL)),
        ],
        out_specs=pl.BlockSpec((TM, D_MODEL), lambda i: (i, 0)),
        scratch_shapes=[
            pltpu.VMEM((nc, 2 * FC // LANES, SUBLANES, LANES), F32),
            pltpu.VMEM((FFN_AHEAD + 1, 2 * FC // LANES, SUBLANES + TM, LANES), F32),
        ],
        compiler_params=_params("arbitrary"),
        name="conv_ffn",
    )(x, g.reshape(1, D_MODEL), w_up, conv_w.T, w_down, g_final.reshape(1, D_MODEL))


def _sc_kernel(x_ref, g_ref, win_ref, cw_ref, wout_ref, o_ref, carry_ref, work_ref, b_ref, *,
               tiles_per_seq, n_chunks):
    slots = work_ref.shape[0]
    h = _rmsnorm(x_ref[...], g_ref[...]).astype(BF16)

    @pl.when(pl.program_id(0) % tiles_per_seq == 0)
    def _():
        carry_ref[...] = jnp.zeros_like(carry_ref)

    def in_proj(j):
        b, c, u = (jnp.dot(h, win_ref[:, part * SC_WIDTH + j * FC:part * SC_WIDTH + (j + 1) * FC].astype(BF16),
                           preferred_element_type=F32) for part in range(3))
        b_ref[j % slots] = b
        _stage_rows(work_ref, j % slots, c * u)

    def out_proj(j):
        conv = _causal_conv3_staged(work_ref, j % slots, carry_ref, j, cw_ref[:, j * FC:(j + 1) * FC])
        y = (b_ref[j % slots] * conv).astype(BF16)
        base = x_ref if j == 0 else o_ref
        o_ref[...] = base[...] + jnp.dot(y, wout_ref[j * FC:(j + 1) * FC, :].astype(BF16),
                                         preferred_element_type=F32)

    for j in range(slots - 1):
        in_proj(j)
    for j in range(n_chunks):
        if j + slots - 1 < n_chunks:
            in_proj(j + slots - 1)
        out_proj(j)


def _short_conv(x, g, w_in, conv_w, w_out, seq):
    m = x.shape[0]
    nc = SC_WIDTH // FC
    kern = functools.partial(_sc_kernel, tiles_per_seq=seq // TM, n_chunks=nc)
    return pl.pallas_call(
        kern,
        out_shape=jax.ShapeDtypeStruct((m, D_MODEL), F32),
        grid=(m // TM,),
        in_specs=[
            pl.BlockSpec((TM, D_MODEL), lambda i: (i, 0)),
            _resident((1, D_MODEL)),
            _resident((D_MODEL, 3 * SC_WIDTH)),
            _resident((3, SC_WIDTH)),
            _resident((SC_WIDTH, D_MODEL)),
        ],
        out_specs=pl.BlockSpec((TM, D_MODEL), lambda i: (i, 0)),
        scratch_shapes=[
            pltpu.VMEM((nc, FC // LANES, SUBLANES, LANES), F32),
            pltpu.VMEM((SC_AHEAD + 1, FC // LANES, SUBLANES + TM, LANES), F32),
            pltpu.VMEM((SC_AHEAD + 1, TM, FC), F32),
        ],
        compiler_params=_params("arbitrary"),
        name="short_conv",
    )(x, g.reshape(1, D_MODEL), w_in, conv_w.T, w_out)


def _chunk_cumsum(x):
    rows, w = x.shape
    per = GLA_CHUNK // SUBLANES
    g = x.reshape(rows // SUBLANES, SUBLANES, w)
    sub = lax.broadcasted_iota(jnp.int32, g.shape, 1)
    shift = 1
    while shift < SUBLANES:
        g = g + jnp.where(sub >= shift, pltpu.roll(g, shift, 1), 0.0)
        shift *= 2
    g = g.reshape(rows // GLA_CHUNK, per, SUBLANES, w)
    tot = jnp.broadcast_to(g[:, :, SUBLANES - 1:SUBLANES, :], g.shape)
    parts = [g[:, 0]]
    off = tot[:, 0]
    for i in range(1, per):
        parts.append(g[:, i] + off)
        off = off + tot[:, i]
    return jnp.stack(parts, axis=1).reshape(rows, w)


def _gla_kernel(x_ref, g_ref, win_ref, wa_ref, wgu_ref, bg_ref, hg_ref, wout_ref, o_ref,
                st_ref, qin_ref, kin_ref, kdec_ref, v_ref, gate_ref, y_ref):
    n_chunks = TB // GLA_CHUNK
    nt = (((1,), (1,)), ((), ()))
    tn = (((0,), (0,)), ((), ()))

    @pl.when(pl.program_id(1) == 0)
    def _():
        st_ref[...] = jnp.zeros_like(st_ref)

    x = x_ref[...]
    h = _rmsnorm(x, g_ref[...]).astype(BF16)

    def proj(start, width):
        return jnp.dot(h, win_ref[:, start:start + width].astype(BF16), preferred_element_type=F32)

    a_low = jnp.dot(h, wa_ref[...], preferred_element_type=F32).astype(BF16)
    z = jnp.dot(a_low, wgu_ref[...], preferred_element_type=F32) + bg_ref[...]
    log_a = -(jnp.maximum(-z, 0.0) + jnp.log1p(jnp.exp(-jnp.abs(z)))) / GLA_GATE_TAU
    bcum = _chunk_cumsum(log_a).reshape(n_chunks, GLA_CHUNK, GLA_KEY_DIM)
    b_last = bcum[:, GLA_CHUNK - 1:, :]
    decay = jnp.exp(b_last)
    to_rows = lambda t: t.reshape(TB, GLA_KEY_DIM)

    q = proj(0, GLA_KEY_DIM) * (GLA_HEAD_K ** -0.5)
    qin_ref[...] = (q * to_rows(jnp.exp(bcum))).astype(BF16)
    k = proj(GLA_KEY_DIM, GLA_KEY_DIM)
    kin_ref[...] = (k * to_rows(jnp.exp(-bcum))).astype(BF16)
    kdec_ref[...] = (k * to_rows(jnp.exp(b_last - bcum))).astype(BF16)
    v_ref[...] = proj(2 * GLA_KEY_DIM, GLA_VAL_DIM).astype(BF16)
    gate_ref[...] = proj(2 * GLA_KEY_DIM + GLA_VAL_DIM, GLA_VAL_DIM)

    r = lax.broadcasted_iota(jnp.int32, (TB, TB), 0)
    c = lax.broadcasted_iota(jnp.int32, (TB, TB), 1)
    keep = (c <= r) & (c >= r - r % GLA_CHUNK)

    for hd in range(GLA_HEADS):
        ks = slice(hd * GLA_HEAD_K, (hd + 1) * GLA_HEAD_K)
        vs = slice(hd * GLA_HEAD_V, (hd + 1) * GLA_HEAD_V)
        qh, kh, vh = qin_ref[:, ks], kin_ref[:, ks], v_ref[:, vs]
        scores = lax.dot_general(qh, kh, nt, preferred_element_type=F32)
        o_intra = jnp.dot(jnp.where(keep, scores, 0.0).astype(BF16), vh, preferred_element_type=F32)
        st = st_ref[hd]
        o_inter = []
        for n in range(n_chunks):
            rows = slice(n * GLA_CHUNK, (n + 1) * GLA_CHUNK)
            o_inter.append(lax.dot_general(qh[rows], st.astype(BF16), nt, preferred_element_type=F32))
            kv_t = lax.dot_general(vh[rows], kdec_ref[rows, ks], tn, preferred_element_type=F32)
            st = decay[n, :, ks] * st + kv_t
        st_ref[hd] = st
        o = o_intra + jnp.concatenate(o_inter, axis=0)
        on = o * lax.rsqrt(jnp.mean(o * o, axis=-1, keepdims=True) + RMS_EPS) * hg_ref[:, vs]
        gate = gate_ref[:, vs]
        y_ref[:, vs] = (on * (gate * jax.nn.sigmoid(gate))).astype(BF16)

    o_ref[...] = x + jnp.dot(y_ref[...], wout_ref[...].astype(BF16), preferred_element_type=F32)


def _gla_mixer(x, g, w_in, w_gate_up, b_gate, head_g, w_out, batch):
    m = x.shape[0]
    tiles = m // batch // TB
    wa = jnp.pad(w_in[:, 2 * GLA_KEY_DIM + 2 * GLA_VAL_DIM:], ((0, 0), (0, LANES - GLA_GATE_RANK))).astype(BF16)
    wgu = jnp.pad(w_gate_up, ((0, LANES - GLA_GATE_RANK), (0, 0))).astype(BF16)
    tile = pl.BlockSpec((TB, D_MODEL), lambda b, t: (b * tiles + t, 0))
    return pl.pallas_call(
        _gla_kernel,
        out_shape=jax.ShapeDtypeStruct((m, D_MODEL), F32),
        grid=(batch, tiles),
        in_specs=[
            tile,
            _resident((1, D_MODEL)),
            _resident(w_in.shape),
            _resident((D_MODEL, LANES)),
            _resident((LANES, GLA_KEY_DIM)),
            _resident((1, GLA_KEY_DIM)),
            _resident((1, GLA_VAL_DIM)),
            _resident((GLA_VAL_DIM, D_MODEL)),
        ],
        out_specs=tile,
        scratch_shapes=[
            pltpu.VMEM((GLA_HEADS, GLA_HEAD_V, GLA_HEAD_K), F32),
            pltpu.VMEM((TB, GLA_KEY_DIM), BF16),
            pltpu.VMEM((TB, GLA_KEY_DIM), BF16),
            pltpu.VMEM((TB, GLA_KEY_DIM), BF16),
            pltpu.VMEM((TB, GLA_VAL_DIM), BF16),
            pltpu.VMEM((TB, GLA_VAL_DIM), F32),
            pltpu.VMEM((TB, GLA_VAL_DIM), BF16),
        ],
        compiler_params=_params("arbitrary", "arbitrary"),
        name="gla_mixer",
    )(x, g.reshape(1, D_MODEL), w_in, wa, wgu, b_gate.reshape(1, GLA_KEY_DIM),
      head_g.reshape(1, GLA_VAL_DIM), w_out)


def kernel(x, norm_mix_g, norm_ffn_g, gla_w_in, gla_w_gate_up, gla_b_gate, gla_head_norm_g, gla_w_out,
           sc_w_in, sc_conv_w, sc_w_out, ffn_w_up, ffn_conv_w, ffn_w_down, final_norm_g):
    batch, seq, d = x.shape
    assert d == D_MODEL and seq % TM == 0 and seq % TB == 0
    xf = x.reshape(batch * seq, d)

    xf = _gla_mixer(xf, norm_mix_g[0], gla_w_in[0], gla_w_gate_up[0], gla_b_gate[0],
                    gla_head_norm_g[0], gla_w_out[0], batch)
    xf = _conv_ffn(xf, norm_ffn_g[0], ffn_w_up, ffn_conv_w[0], ffn_w_down, 0, final_norm_g, False, seq)
    xf = _short_conv(xf, norm_mix_g[1], sc_w_in[0], sc_conv_w[0], sc_w_out[0], seq)
    xf = _conv_ffn(xf, norm_ffn_g[1], ffn_w_up, ffn_conv_w[1], ffn_w_down, 1, final_norm_g, True, seq)
    return xf.reshape(batch, seq, d)
```

```python
import functools

import jax
import jax.numpy as jnp
from jax import lax
from jax.experimental import pallas as pl
from jax.experimental.pallas import tpu as pltpu

F32 = jnp.float32
BF16 = jnp.bfloat16

D_MODEL = 1024
GLA_HEADS = 4
GLA_KEY_DIM = D_MODEL // 2
GLA_VAL_DIM = D_MODEL
GLA_HEAD_K = GLA_KEY_DIM // GLA_HEADS
GLA_HEAD_V = GLA_VAL_DIM // GLA_HEADS
GLA_GATE_RANK = 16
GLA_GATE_TAU = 16.0
GLA_CHUNK = 64
SC_WIDTH = D_MODEL
D_FF = ((8 * D_MODEL // 3 + 255) // 256) * 256
RMS_EPS = 1e-6

SUBLANES = 8
LANES = 128
VMEM_LIMIT = 56 * 1024 * 1024

TM = 512
TB = 1024
SB = 256
FC = 256
FFN_AHEAD = 2
SC_AHEAD = 1


def _rmsnorm(x, g):
    return x * lax.rsqrt(jnp.mean(x * x, axis=-1, keepdims=True) + RMS_EPS) * g


def _stage_rows(work_ref, slot, val, first=0):
    for b in range(val.shape[1] // LANES):
        work_ref[slot, first + b, SUBLANES:, :] = val[:, b * LANES:(b + 1) * LANES]


def _causal_conv3_staged(work_ref, slot, carry_ref, j, cw):
    rows = work_ref.shape[2] - SUBLANES
    outs = []
    for b in range(work_ref.shape[1]):
        work_ref[slot, b, :SUBLANES, :] = carry_ref[j, b]
        carry_ref[j, b] = work_ref[slot, b, rows:, :]
        lanes = slice(b * LANES, (b + 1) * LANES)
        outs.append(cw[0:1, lanes] * work_ref[slot, b, SUBLANES - 2:SUBLANES - 2 + rows, :]
                    + cw[1:2, lanes] * work_ref[slot, b, SUBLANES - 1:SUBLANES - 1 + rows, :]
                    + cw[2:3, lanes] * work_ref[slot, b, SUBLANES:, :])
    return jnp.concatenate(outs, axis=1)


def _resident(shape):
    nd = len(shape)
    return pl.BlockSpec(shape, lambda *_: (0,) * nd, pipeline_mode=pl.Buffered(1))


def _resident_layer(shape, layer):
    nd = len(shape)
    return pl.BlockSpec((None,) + tuple(shape), lambda *_: (layer,) + (0,) * nd, pipeline_mode=pl.Buffered(1))


def _params(*sem):
    return pltpu.CompilerParams(dimension_semantics=sem, vmem_limit_bytes=VMEM_LIMIT)


def _ffn_kernel(x_ref, g_ref, wup_ref, cw_ref, wdn_ref, gf_ref, o_ref, carry_ref, work_ref, *,
                tiles_per_seq, n_chunks, final_norm):
    slots = work_ref.shape[0]
    h = _rmsnorm(x_ref[...], g_ref[...]).astype(BF16)

    @pl.when(pl.program_id(0) % tiles_per_seq == 0)
    def _():
        carry_ref[...] = jnp.zeros_like(carry_ref)

    def cols(j):
        return slice(j * FC, (j + 1) * FC), slice(D_FF + j * FC, D_FF + (j + 1) * FC)

    def up_proj(j):
        for part, cs in enumerate(cols(j)):
            _stage_rows(work_ref, j % slots, jnp.dot(h, wup_ref[:, cs].astype(BF16), preferred_element_type=F32),
                        part * FC // LANES)

    def down_proj(j):
        cw = jnp.concatenate([cw_ref[:, cs] for cs in cols(j)], axis=1)
        c = _causal_conv3_staged(work_ref, j % slots, carry_ref, j, cw)
        a, u = c[:, :FC], c[:, FC:]
        act = (a * jax.nn.sigmoid(a) * u).astype(BF16)
        base = x_ref if j == 0 else o_ref
        o_ref[...] = base[...] + jnp.dot(act, wdn_ref[j * FC:(j + 1) * FC, :].astype(BF16),
                                         preferred_element_type=F32)

    for j in range(slots - 1):
        up_proj(j)
    for j in range(n_chunks):
        if j + slots - 1 < n_chunks:
            up_proj(j + slots - 1)
        down_proj(j)

    if final_norm:
        o_ref[...] = _rmsnorm(o_ref[...], gf_ref[...])


def _conv_ffn(x, g, w_up, conv_w, w_down, layer, g_final, final_norm, seq):
    m = x.shape[0]
    nc = D_FF // FC
    kern = functools.partial(_ffn_kernel, tiles_per_seq=seq // TM, n_chunks=nc, final_norm=final_norm)
    return pl.pallas_call(
        kern,
        out_shape=jax.ShapeDtypeStruct((m, D_MODEL), F32),
        grid=(m // TM,),
        in_specs=[
            pl.BlockSpec((TM, D_MODEL), lambda i: (i, 0)),
            _resident((1, D_MODEL)),
            _resident_layer((D_MODEL, 2 * D_FF), layer),
            _resident((3, 2 * D_FF)),
            _resident_layer((D_FF, D_MODEL), layer),
            _resident((1, D_MODEL)),
        ],
        out_specs=pl.BlockSpec((TM, D_MODEL), lambda i: (i, 0)),
        scratch_shapes=[
            pltpu.VMEM((nc, 2 * FC // LANES, SUBLANES, LANES), F32),
            pltpu.VMEM((FFN_AHEAD + 1, 2 * FC // LANES, SUBLANES + TM, LANES), F32),
        ],
        compiler_params=_params("arbitrary"),
        name="conv_ffn",
    )(x, g.reshape(1, D_MODEL), w_up, conv_w.T, w_down, g_final.reshape(1, D_MODEL))


def _sc_kernel(x_ref, g_ref, win_ref, cw_ref, wout_ref, o_ref, carry_ref, work_ref, b_ref, *,
               tiles_per_seq, n_chunks):
    slots = work_ref.shape[0]
    h = _rmsnorm(x_ref[...], g_ref[...]).astype(BF16)

    @pl.when(pl.program_id(0) % tiles_per_seq == 0)
    def _():
        carry_ref[...] = jnp.zeros_like(carry_ref)

    def in_proj(j):
        b, c, u = (jnp.dot(h, win_ref[:, part * SC_WIDTH + j * FC:part * SC_WIDTH + (j + 1) * FC].astype(BF16),
                           preferred_element_type=F32) for part in range(3))
        b_ref[j % slots] = b
        _stage_rows(work_ref, j % slots, c * u)

    def out_proj(j):
        conv = _causal_conv3_staged(work_ref, j % slots, carry_ref, j, cw_ref[:, j * FC:(j + 1) * FC])
        y = (b_ref[j % slots] * conv).astype(BF16)
        base = x_ref if j == 0 else o_ref
        o_ref[...] = base[...] + jnp.dot(y, wout_ref[j * FC:(j + 1) * FC, :].astype(BF16),
                                         preferred_element_type=F32)

    for j in range(slots - 1):
        in_proj(j)
    for j in range(n_chunks):
        if j + slots - 1 < n_chunks:
            in_proj(j + slots - 1)
        out_proj(j)


def _short_conv(x, g, w_in, conv_w, w_out, seq):
    m = x.shape[0]
    nc = SC_WIDTH // FC
    kern = functools.partial(_sc_kernel, tiles_per_seq=seq // TM, n_chunks=nc)
    return pl.pallas_call(
        kern,
        out_shape=jax.ShapeDtypeStruct((m, D_MODEL), F32),
        grid=(m // TM,),
        in_specs=[
            pl.BlockSpec((TM, D_MODEL), lambda i: (i, 0)),
            _resident((1, D_MODEL)),
            _resident((D_MODEL, 3 * SC_WIDTH)),
            _resident((3, SC_WIDTH)),
            _resident((SC_WIDTH, D_MODEL)),
        ],
        out_specs=pl.BlockSpec((TM, D_MODEL), lambda i: (i, 0)),
        scratch_shapes=[
            pltpu.VMEM((nc, FC // LANES, SUBLANES, LANES), F32),
            pltpu.VMEM((SC_AHEAD + 1, FC // LANES, SUBLANES + TM, LANES), F32),
            pltpu.VMEM((SC_AHEAD + 1, TM, FC), F32),
        ],
        compiler_params=_params("arbitrary"),
        name="short_conv",
    )(x, g.reshape(1, D_MODEL), w_in, conv_w.T, w_out)


def _chunk_cumsum(x):
    rows, w = x.shape
    per = GLA_CHUNK // SUBLANES
    g = x.reshape(rows // SUBLANES, SUBLANES, w)
    sub = lax.broadcasted_iota(jnp.int32, g.shape, 1)
    shift = 1
    while shift < SUBLANES:
        g = g + jnp.where(sub >= shift, pltpu.roll(g, shift, 1), 0.0)
        shift *= 2
    g = g.reshape(rows // GLA_CHUNK, per, SUBLANES, w)
    tot = jnp.broadcast_to(g[:, :, SUBLANES - 1:SUBLANES, :], g.shape)
    parts = [g[:, 0]]
    off = tot[:, 0]
    for i in range(1, per):
        parts.append(g[:, i] + off)
        off = off + tot[:, i]
    return jnp.stack(parts, axis=1).reshape(rows, w)


def _gla_kernel(x_ref, g_ref, win_ref, wa_ref, wgu_ref, bg_ref, hg_ref, wout_ref, o_ref,
                st_ref, q_ref, k_ref, qin_ref, kin_ref, kdec_ref, v_ref, gate_ref, y_ref, winb_ref, woutb_ref):
    n_chunks = SB // GLA_CHUNK
    nt = (((1,), (1,)), ((), ()))
    tn = (((0,), (0,)), ((), ()))
    heads = range(GLA_HEADS)
    ks = [slice(hd * GLA_HEAD_K, (hd + 1) * GLA_HEAD_K) for hd in heads]
    vs = [slice(hd * GLA_HEAD_V, (hd + 1) * GLA_HEAD_V) for hd in heads]
    rows = [slice(n * GLA_CHUNK, (n + 1) * GLA_CHUNK) for n in range(n_chunks)]

    @pl.when(pl.program_id(1) == 0)
    def _():
        st_ref[...] = jnp.zeros_like(st_ref)

    @pl.when((pl.program_id(0) == 0) & (pl.program_id(1) == 0))
    def _():
        winb_ref[...] = win_ref[:, :winb_ref.shape[1]].astype(BF16)
        woutb_ref[...] = wout_ref[...].astype(BF16)

    r = lax.broadcasted_iota(jnp.int32, (SB, SB), 0)
    c = lax.broadcasted_iota(jnp.int32, (SB, SB), 1)
    keep = (c <= r) & (c >= r - r % GLA_CHUNK)

    for sb in range(TB // SB):
        blk = slice(sb * SB, (sb + 1) * SB)
        h = _rmsnorm(x_ref[blk, :], g_ref[...]).astype(BF16)

        def proj(start, width):
            return jnp.dot(h, winb_ref[:, start:start + width], preferred_element_type=F32)

        a_low = jnp.dot(h, wa_ref[...], preferred_element_type=F32).astype(BF16)
        q_ref[...] = proj(0, GLA_KEY_DIM) * (GLA_HEAD_K ** -0.5)
        k_ref[...] = proj(GLA_KEY_DIM, GLA_KEY_DIM)
        z = jnp.dot(a_low, wgu_ref[...], preferred_element_type=F32) + bg_ref[...]
        v_ref[...] = proj(2 * GLA_KEY_DIM, GLA_VAL_DIM).astype(BF16)
        gate_ref[...] = proj(2 * GLA_KEY_DIM + GLA_VAL_DIM, GLA_VAL_DIM)

        log_a = -(jnp.maximum(-z, 0.0) + jnp.log1p(jnp.exp(-jnp.abs(z)))) / GLA_GATE_TAU
        bcum = _chunk_cumsum(log_a).reshape(n_chunks, GLA_CHUNK, GLA_KEY_DIM)
        b_last = bcum[:, GLA_CHUNK - 1:, :]
        decay = jnp.exp(b_last)
        to_rows = lambda t: t.reshape(SB, GLA_KEY_DIM)

        k = k_ref[...]
        kdec_ref[...] = (k * to_rows(jnp.exp(b_last - bcum))).astype(BF16)
        kin_ref[...] = (k * to_rows(jnp.exp(-bcum))).astype(BF16)
        qin_ref[...] = (q_ref[...] * to_rows(jnp.exp(bcum))).astype(BF16)

        kv_t = [[lax.dot_general(v_ref[rows[n], vs[hd]], kdec_ref[rows[n], ks[hd]], tn, preferred_element_type=F32)
                 for hd in heads] for n in range(n_chunks)]
        scores = [lax.dot_general(qin_ref[:, ks[hd]], kin_ref[:, ks[hd]], nt, preferred_element_type=F32)
                  for hd in heads]
        probs = [jnp.where(keep, scores[hd], 0.0).astype(BF16) for hd in heads]
        o_intra = [jnp.dot(probs[hd], v_ref[:, vs[hd]], preferred_element_type=F32) for hd in heads]

        st = [st_ref[hd] for hd in heads]
        o_inter = [[None] * n_chunks for _ in heads]
        for n in range(n_chunks):
            for hd in heads:
                o_inter[hd][n] = lax.dot_general(qin_ref[rows[n], ks[hd]], st[hd].astype(BF16), nt,
                                                 preferred_element_type=F32)
            for hd in heads:
                st[hd] = decay[n, :, ks[hd]] * st[hd] + kv_t[n][hd]
        for hd in heads:
            st_ref[hd] = st[hd]

        for hd in heads:
            o = o_intra[hd] + jnp.concatenate(o_inter[hd], axis=0)
            on = o * lax.rsqrt(jnp.mean(o * o, axis=-1, keepdims=True) + RMS_EPS) * hg_ref[:, vs[hd]]
            gate = gate_ref[:, vs[hd]]
            y_ref[:, vs[hd]] = (on * (gate * jax.nn.sigmoid(gate))).astype(BF16)

        o_ref[blk, :] = x_ref[blk, :] + jnp.dot(y_ref[...], woutb_ref[...], preferred_element_type=F32)


def _gla_mixer(x, g, w_in, w_gate_up, b_gate, head_g, w_out, batch):
    m = x.shape[0]
    tiles = m // batch // TB
    wa = jnp.pad(w_in[:, 2 * GLA_KEY_DIM + 2 * GLA_VAL_DIM:], ((0, 0), (0, LANES - GLA_GATE_RANK))).astype(BF16)
    wgu = jnp.pad(w_gate_up, ((0, LANES - GLA_GATE_RANK), (0, 0))).astype(BF16)
    tile = pl.BlockSpec((TB, D_MODEL), lambda b, t: (b * tiles + t, 0))
    return pl.pallas_call(
        _gla_kernel,
        out_shape=jax.ShapeDtypeStruct((m, D_MODEL), F32),
        grid=(batch, tiles),
        in_specs=[
            tile,
            _resident((1, D_MODEL)),
            _resident(w_in.shape),
            _resident((D_MODEL, LANES)),
            _resident((LANES, GLA_KEY_DIM)),
            _resident((1, GLA_KEY_DIM)),
            _resident((1, GLA_VAL_DIM)),
            _resident((GLA_VAL_DIM, D_MODEL)),
        ],
        out_specs=tile,
        scratch_shapes=[
            pltpu.VMEM((GLA_HEADS, GLA_HEAD_V, GLA_HEAD_K), F32),
            pltpu.VMEM((SB, GLA_KEY_DIM), F32),
            pltpu.VMEM((SB, GLA_KEY_DIM), F32),
            pltpu.VMEM((SB, GLA_KEY_DIM), BF16),
            pltpu.VMEM((SB, GLA_KEY_DIM), BF16),
            pltpu.VMEM((SB, GLA_KEY_DIM), BF16),
            pltpu.VMEM((SB, GLA_VAL_DIM), BF16),
            pltpu.VMEM((SB, GLA_VAL_DIM), F32),
            pltpu.VMEM((SB, GLA_VAL_DIM), BF16),
            pltpu.VMEM((D_MODEL, 2 * GLA_KEY_DIM + 2 * GLA_VAL_DIM), BF16),
            pltpu.VMEM((GLA_VAL_DIM, D_MODEL), BF16),
        ],
        compiler_params=_params("arbitrary", "arbitrary"),
        name="gla_mixer",
    )(x, g.reshape(1, D_MODEL), w_in, wa, wgu, b_gate.reshape(1, GLA_KEY_DIM),
      head_g.reshape(1, GLA_VAL_DIM), w_out)


def kernel(x, norm_mix_g, norm_ffn_g, gla_w_in, gla_w_gate_up, gla_b_gate, gla_head_norm_g, gla_w_out,
           sc_w_in, sc_conv_w, sc_w_out, ffn_w_up, ffn_conv_w, ffn_w_down, final_norm_g):
    batch, seq, d = x.shape
    assert d == D_MODEL and seq % TM == 0 and seq % TB == 0
    xf = x.reshape(batch * seq, d)

    xf = _gla_mixer(xf, norm_mix_g[0], gla_w_in[0], gla_w_gate_up[0], gla_b_gate[0],
                    gla_head_norm_g[0], gla_w_out[0], batch)
    xf = _conv_ffn(xf, norm_ffn_g[0], ffn_w_up, ffn_conv_w[0], ffn_w_down, 0, final_norm_g, False, seq)
    xf = _short_conv(xf, norm_mix_g[1], sc_w_in[0], sc_conv_w[0], sc_w_out[0], seq)
    xf = _conv_ffn(xf, norm_ffn_g[1], ffn_w_up, ffn_conv_w[1], ffn_w_down, 1, final_norm_g, True, seq)
    return xf.reshape(batch, seq, d)
```

```python
import functools

import jax
import jax.numpy as jnp
from jax import lax
from jax.experimental import pallas as pl
from jax.experimental.pallas import tpu as pltpu

F32 = jnp.float32
BF16 = jnp.bfloat16

D_MODEL = 1024
GLA_HEADS = 4
GLA_KEY_DIM = D_MODEL // 2
GLA_VAL_DIM = D_MODEL
GLA_HEAD_K = GLA_KEY_DIM // GLA_HEADS
GLA_HEAD_V = GLA_VAL_DIM // GLA_HEADS
GLA_GATE_RANK = 16
GLA_GATE_TAU = 16.0
GLA_CHUNK = 64
SC_WIDTH = D_MODEL
D_FF = ((8 * D_MODEL // 3 + 255) // 256) * 256
RMS_EPS = 1e-6

SUBLANES = 8
LANES = 128
VMEM_LIMIT = 56 * 1024 * 1024

TM = 512
TB = 1024
SB = 256
SC_TM = 1024
FC = 256
FFN_AHEAD = 2
SC_AHEAD = 1


def _rmsnorm(x, g):
    return x * lax.rsqrt(jnp.mean(x * x, axis=-1, keepdims=True) + RMS_EPS) * g


def _stage_rows(work_ref, slot, val, first=0):
    for b in range(val.shape[1] // LANES):
        work_ref[slot, first + b, SUBLANES:, :] = val[:, b * LANES:(b + 1) * LANES]


def _causal_conv3_staged(work_ref, slot, carry_ref, j, cw):
    rows = work_ref.shape[2] - SUBLANES
    outs = []
    for b in range(work_ref.shape[1]):
        work_ref[slot, b, :SUBLANES, :] = carry_ref[j, b]
        carry_ref[j, b] = work_ref[slot, b, rows:, :]
        lanes = slice(b * LANES, (b + 1) * LANES)
        outs.append(cw[0:1, lanes] * work_ref[slot, b, SUBLANES - 2:SUBLANES - 2 + rows, :]
                    + cw[1:2, lanes] * work_ref[slot, b, SUBLANES - 1:SUBLANES - 1 + rows, :]
                    + cw[2:3, lanes] * work_ref[slot, b, SUBLANES:, :])
    return jnp.concatenate(outs, axis=1)


def _resident(shape):
    nd = len(shape)
    return pl.BlockSpec(shape, lambda *_: (0,) * nd, pipeline_mode=pl.Buffered(1))


def _resident_layer(shape, layer):
    nd = len(shape)
    return pl.BlockSpec((None,) + tuple(shape), lambda *_: (layer,) + (0,) * nd, pipeline_mode=pl.Buffered(1))


def _params(*sem):
    return pltpu.CompilerParams(dimension_semantics=sem, vmem_limit_bytes=VMEM_LIMIT)


def _ffn_kernel(x_ref, g_ref, wup_ref, cw_ref, wdn_ref, gf_ref, o_ref, carry_ref, work_ref, *,
                tiles_per_seq, n_chunks, final_norm):
    slots = work_ref.shape[0]
    h = _rmsnorm(x_ref[...], g_ref[...]).astype(BF16)

    @pl.when(pl.program_id(0) % tiles_per_seq == 0)
    def _():
        carry_ref[...] = jnp.zeros_like(carry_ref)

    def cols(j):
        return slice(j * FC, (j + 1) * FC), slice(D_FF + j * FC, D_FF + (j + 1) * FC)

    def up_proj(j):
        for part, cs in enumerate(cols(j)):
            _stage_rows(work_ref, j % slots, jnp.dot(h, wup_ref[:, cs].astype(BF16), preferred_element_type=F32),
                        part * FC // LANES)

    def down_proj(j):
        cw = jnp.concatenate([cw_ref[:, cs] for cs in cols(j)], axis=1)
        c = _causal_conv3_staged(work_ref, j % slots, carry_ref, j, cw)
        a, u = c[:, :FC], c[:, FC:]
        act = (a * jax.nn.sigmoid(a) * u).astype(BF16)
        base = x_ref if j == 0 else o_ref
        o_ref[...] = base[...] + jnp.dot(act, wdn_ref[j * FC:(j + 1) * FC, :].astype(BF16),
                                         preferred_element_type=F32)

    for j in range(slots - 1):
        up_proj(j)
    for j in range(n_chunks):
        if j + slots - 1 < n_chunks:
            up_proj(j + slots - 1)
        down_proj(j)

    if final_norm:
        o_ref[...] = _rmsnorm(o_ref[...], gf_ref[...])


def _conv_ffn(x, g, w_up, conv_w, w_down, layer, g_final, final_norm, seq):
    m = x.shape[0]
    nc = D_FF // FC
    kern = functools.partial(_ffn_kernel, tiles_per_seq=seq // TM, n_chunks=nc, final_norm=final_norm)
    return pl.pallas_call(
        kern,
        out_shape=jax.ShapeDtypeStruct((m, D_MODEL), F32),
        grid=(m // TM,),
        in_specs=[
            pl.BlockSpec((TM, D_MODEL), lambda i: (i, 0)),
            _resident((1, D_MODEL)),
            _resident_layer((D_MODEL, 2 * D_FF), layer),
            _resident((3, 2 * D_FF)),
            _resident_layer((D_FF, D_MODEL), layer),
            _resident((1, D_MODEL)),
        ],
        out_specs=pl.BlockSpec((TM, D_MODEL), lambda i: (i, 0)),
        scratch_shapes=[
            pltpu.VMEM((nc, 2 * FC // LANES, SUBLANES, LANES), F32),
            pltpu.VMEM((FFN_AHEAD + 1, 2 * FC // LANES, SUBLANES + TM, LANES), F32),
        ],
        compiler_params=_params("arbitrary"),
        name="conv_ffn",
    )(x, g.reshape(1, D_MODEL), w_up, conv_w.T, w_down, g_final.reshape(1, D_MODEL))


def _sc_kernel(x_ref, g_ref, win_ref, cw_ref, wout_ref, o_ref, carry_ref, work_ref, b_ref, *,
               tiles_per_seq, n_chunks):
    slots = work_ref.shape[0]
    h = _rmsnorm(x_ref[...], g_ref[...]).astype(BF16)

    @pl.when(pl.program_id(0) % tiles_per_seq == 0)
    def _():
        carry_ref[...] = jnp.zeros_like(carry_ref)

    def in_proj(j):
        b, c, u = (jnp.dot(h, win_ref[:, part * SC_WIDTH + j * FC:part * SC_WIDTH + (j + 1) * FC].astype(BF16),
                           preferred_element_type=F32) for part in range(3))
        b_ref[j % slots] = b
        _stage_rows(work_ref, j % slots, c * u)

    def out_proj(j):
        conv = _causal_conv3_staged(work_ref, j % slots, carry_ref, j, cw_ref[:, j * FC:(j + 1) * FC])
        y = (b_ref[j % slots] * conv).astype(BF16)
        base = x_ref if j == 0 else o_ref
        o_ref[...] = base[...] + jnp.dot(y, wout_ref[j * FC:(j + 1) * FC, :].astype(BF16),
                                         preferred_element_type=F32)

    for j in range(slots - 1):
        in_proj(j)
    for j in range(n_chunks):
        if j + slots - 1 < n_chunks:
            in_proj(j + slots - 1)
        out_proj(j)


def _short_conv(x, g, w_in, conv_w, w_out, seq):
    m = x.shape[0]
    nc = SC_WIDTH // FC
    kern = functools.partial(_sc_kernel, tiles_per_seq=seq // SC_TM, n_chunks=nc)
    return pl.pallas_call(
        kern,
        out_shape=jax.ShapeDtypeStruct((m, D_MODEL), F32),
        grid=(m // SC_TM,),
        in_specs=[
            pl.BlockSpec((SC_TM, D_MODEL), lambda i: (i, 0)),
            _resident((1, D_MODEL)),
            _resident((D_MODEL, 3 * SC_WIDTH)),
            _resident((3, SC_WIDTH)),
            _resident((SC_WIDTH, D_MODEL)),
        ],
        out_specs=pl.BlockSpec((SC_TM, D_MODEL), lambda i: (i, 0)),
        scratch_shapes=[
            pltpu.VMEM((nc, FC // LANES, SUBLANES, LANES), F32),
            pltpu.VMEM((SC_AHEAD + 1, FC // LANES, SUBLANES + SC_TM, LANES), F32),
            pltpu.VMEM((SC_AHEAD + 1, SC_TM, FC), F32),
        ],
        compiler_params=_params("arbitrary"),
        name="short_conv",
    )(x, g.reshape(1, D_MODEL), w_in, conv_w.T, w_out)


def _chunk_cumsum(x):
    rows, w = x.shape
    per = GLA_CHUNK // SUBLANES
    g = x.reshape(rows // SUBLANES, SUBLANES, w)
    sub = lax.broadcasted_iota(jnp.int32, g.shape, 1)
    shift = 1
    while shift < SUBLANES:
        g = g + jnp.where(sub >= shift, pltpu.roll(g, shift, 1), 0.0)
        shift *= 2
    g = g.reshape(rows // GLA_CHUNK, per, SUBLANES, w)
    tot = jnp.broadcast_to(g[:, :, SUBLANES - 1:SUBLANES, :], g.shape)
    parts = [g[:, 0]]
    off = tot[:, 0]
    for i in range(1, per):
        parts.append(g[:, i] + off)
        off = off + tot[:, i]
    return jnp.stack(parts, axis=1).reshape(rows, w)


def _gla_kernel(x_ref, g_ref, wint_ref, wgu_ref, bg_ref, hg_ref, wout_ref, o_ref,
                st_ref, q_ref, k_ref, qin_ref, kin_ref, kdec_ref, v_ref, gate_ref, y_ref, winb_ref, wab_ref,
                woutb_ref):
    n_chunks = SB // GLA_CHUNK
    nt = (((1,), (1,)), ((), ()))
    tn = (((0,), (0,)), ((), ()))
    heads = range(GLA_HEADS)
    ks = [slice(hd * GLA_HEAD_K, (hd + 1) * GLA_HEAD_K) for hd in heads]
    vs = [slice(hd * GLA_HEAD_V, (hd + 1) * GLA_HEAD_V) for hd in heads]
    rows = [slice(n * GLA_CHUNK, (n + 1) * GLA_CHUNK) for n in range(n_chunks)]

    @pl.when(pl.program_id(1) == 0)
    def _():
        st_ref[...] = jnp.zeros_like(st_ref)

    @pl.when((pl.program_id(0) == 0) & (pl.program_id(1) == 0))
    def _():
        for c0 in range(0, winb_ref.shape[1], LANES):
            winb_ref[:, c0:c0 + LANES] = wint_ref[c0:c0 + LANES, :].T.astype(BF16)
        wa_t = jnp.concatenate([wint_ref[winb_ref.shape[1]:, :],
                                jnp.zeros((LANES - GLA_GATE_RANK, D_MODEL), F32)], axis=0)
        wab_ref[...] = wa_t.T.astype(BF16)
        woutb_ref[...] = wout_ref[...].astype(BF16)

    r = lax.broadcasted_iota(jnp.int32, (SB, SB), 0)
    c = lax.broadcasted_iota(jnp.int32, (SB, SB), 1)
    keep = (c <= r) & (c >= r - r % GLA_CHUNK)

    for sb in range(TB // SB):
        blk = slice(sb * SB, (sb + 1) * SB)
        h = _rmsnorm(x_ref[blk, :], g_ref[...]).astype(BF16)

        def proj(start, width):
            return jnp.dot(h, winb_ref[:, start:start + width], preferred_element_type=F32)

        a_low = jnp.dot(h, wab_ref[...], preferred_element_type=F32).astype(BF16)
        q_ref[...] = proj(0, GLA_KEY_DIM) * (GLA_HEAD_K ** -0.5)
        k_ref[...] = proj(GLA_KEY_DIM, GLA_KEY_DIM)
        z = jnp.dot(a_low, wgu_ref[...], preferred_element_type=F32) + bg_ref[...]
        v_ref[...] = proj(2 * GLA_KEY_DIM, GLA_VAL_DIM).astype(BF16)
        gate_ref[...] = proj(2 * GLA_KEY_DIM + GLA_VAL_DIM, GLA_VAL_DIM)

        log_a = (jnp.minimum(z, 0.0) - jnp.log(1.0 + jnp.exp(-jnp.abs(z)))) * (1.0 / GLA_GATE_TAU)
        bcum = _chunk_cumsum(log_a).reshape(n_chunks, GLA_CHUNK, GLA_KEY_DIM)
        b_last = bcum[:, GLA_CHUNK - 1:, :]
        decay = jnp.exp(b_last)
        to_rows = lambda t: t.reshape(SB, GLA_KEY_DIM)

        k = k_ref[...]
        kdec_ref[...] = (k * to_rows(jnp.exp(b_last - bcum))).astype(BF16)
        kin_ref[...] = (k * to_rows(jnp.exp(-bcum))).astype(BF16)
        qin_ref[...] = (q_ref[...] * to_rows(jnp.exp(bcum))).astype(BF16)

        kv_t = [[lax.dot_general(v_ref[rows[n], vs[hd]], kdec_ref[rows[n], ks[hd]], tn, preferred_element_type=F32)
                 for hd in heads] for n in range(n_chunks)]
        scores = [lax.dot_general(qin_ref[:, ks[hd]], kin_ref[:, ks[hd]], nt, preferred_element_type=F32)
                  for hd in heads]
        probs = [jnp.where(keep, scores[hd], 0.0).astype(BF16) for hd in heads]
        o_intra = [jnp.dot(probs[hd], v_ref[:, vs[hd]], preferred_element_type=F32) for hd in heads]

        st = [st_ref[hd] for hd in heads]
        o_inter = [[None] * n_chunks for _ in heads]
        for n in range(n_chunks):
            for hd in heads:
                o_inter[hd][n] = lax.dot_general(qin_ref[rows[n], ks[hd]], st[hd].astype(BF16), nt,
                                                 preferred_element_type=F32)
            for hd in heads:
                st[hd] = decay[n, :, ks[hd]] * st[hd] + kv_t[n][hd]
        for hd in heads:
            st_ref[hd] = st[hd]

        for hd in heads:
            o = o_intra[hd] + jnp.concatenate(o_inter[hd], axis=0)
            on = o * lax.rsqrt(jnp.mean(o * o, axis=-1, keepdims=True) + RMS_EPS) * hg_ref[:, vs[hd]]
            gate = gate_ref[:, vs[hd]]
            y_ref[:, vs[hd]] = (on * (gate * jax.nn.sigmoid(gate))).astype(BF16)

        o_ref[blk, :] = x_ref[blk, :] + jnp.dot(y_ref[...], woutb_ref[...], preferred_element_type=F32)


def _gla_mixer(x, g, w_in, w_gate_up, b_gate, head_g, w_out, batch):
    m = x.shape[0]
    tiles = m // batch // TB
    wgu = jnp.pad(w_gate_up, ((0, LANES - GLA_GATE_RANK), (0, 0))).astype(BF16)
    tile = pl.BlockSpec((TB, D_MODEL), lambda b, t: (b * tiles + t, 0))
    return pl.pallas_call(
        _gla_kernel,
        out_shape=jax.ShapeDtypeStruct((m, D_MODEL), F32),
        grid=(batch, tiles),
        in_specs=[
            tile,
            _resident((1, D_MODEL)),
            _resident(w_in.T.shape),
            _resident((LANES, GLA_KEY_DIM)),
            _resident((1, GLA_KEY_DIM)),
            _resident((1, GLA_VAL_DIM)),
            _resident((GLA_VAL_DIM, D_MODEL)),
        ],
        out_specs=tile,
        scratch_shapes=[
            pltpu.VMEM((GLA_HEADS, GLA_HEAD_V, GLA_HEAD_K), F32),
            pltpu.VMEM((SB, GLA_KEY_DIM), F32),
            pltpu.VMEM((SB, GLA_KEY_DIM), F32),
            pltpu.VMEM((SB, GLA_KEY_DIM), BF16),
            pltpu.VMEM((SB, GLA_KEY_DIM), BF16),
            pltpu.VMEM((SB, GLA_KEY_DIM), BF16),
            pltpu.VMEM((SB, GLA_VAL_DIM), BF16),
            pltpu.VMEM((SB, GLA_VAL_DIM), F32),
            pltpu.VMEM((SB, GLA_VAL_DIM), BF16),
            pltpu.VMEM((D_MODEL, 2 * GLA_KEY_DIM + 2 * GLA_VAL_DIM), BF16),
            pltpu.VMEM((D_MODEL, LANES), BF16),
            pltpu.VMEM((GLA_VAL_DIM, D_MODEL), BF16),
        ],
        compiler_params=_params("arbitrary", "arbitrary"),
        name="gla_mixer",
    )(x, g.reshape(1, D_MODEL), w_in.T, wgu, b_gate.reshape(1, GLA_KEY_DIM),
      head_g.reshape(1, GLA_VAL_DIM), w_out)


def kernel(x, norm_mix_g, norm_ffn_g, gla_w_in, gla_w_gate_up, gla_b_gate, gla_head_norm_g, gla_w_out,
           sc_w_in, sc_conv_w, sc_w_out, ffn_w_up, ffn_conv_w, ffn_w_down, final_norm_g):
    batch, seq, d = x.shape
    assert d == D_MODEL and seq % TM == 0 and seq % SC_TM == 0 and seq % TB == 0
    xf = x.reshape(batch * seq, d)

    xf = _gla_mixer(xf, norm_mix_g[0], gla_w_in[0], gla_w_gate_up[0], gla_b_gate[0],
                    gla_head_norm_g[0], gla_w_out[0], batch)
    xf = _conv_ffn(xf, norm_ffn_g[0], ffn_w_up, ffn_conv_w[0], ffn_w_down, 0, final_norm_g, False, seq)
    xf = _short_conv(xf, norm_mix_g[1], sc_w_in[0], sc_conv_w[0], sc_w_out[0], seq)
    xf = _conv_ffn(xf, norm_ffn_g[1], ffn_w_up, ffn_conv_w[1], ffn_w_down, 1, final_norm_g, True, seq)
    return xf.reshape(batch, seq, d)
```

```python
import functools

import jax
import jax.numpy as jnp
from jax import lax
from jax.experimental import pallas as pl
from jax.experimental.pallas import tpu as pltpu

F32 = jnp.float32
BF16 = jnp.bfloat16

D_MODEL = 1024
GLA_HEADS = 4
GLA_KEY_DIM = D_MODEL // 2
GLA_VAL_DIM = D_MODEL
GLA_HEAD_K = GLA_KEY_DIM // GLA_HEADS
GLA_HEAD_V = GLA_VAL_DIM // GLA_HEADS
GLA_GATE_RANK = 16
GLA_GATE_TAU = 16.0
GLA_CHUNK = 64
SC_WIDTH = D_MODEL
D_FF = ((8 * D_MODEL // 3 + 255) // 256) * 256
RMS_EPS = 1e-6

SUBLANES = 8
LANES = 128
VMEM_LIMIT = 56 * 1024 * 1024

TM = 512
TB = 1024
SB = 256
SC_TM = 512
FC = 256
FFN_AHEAD = 2
SC_AHEAD = 1


def _rmsnorm(x, g):
    return x * lax.rsqrt(jnp.mean(x * x, axis=-1, keepdims=True) + RMS_EPS) * g


def _stage_rows(work_ref, slot, val, first=0):
    for b in range(val.shape[1] // LANES):
        work_ref[slot, first + b, SUBLANES:, :] = val[:, b * LANES:(b + 1) * LANES]


def _causal_conv3_staged(work_ref, slot, carry_ref, j, cw):
    rows = work_ref.shape[2] - SUBLANES
    outs = []
    for b in range(work_ref.shape[1]):
        work_ref[slot, b, :SUBLANES, :] = carry_ref[j, b]
        carry_ref[j, b] = work_ref[slot, b, rows:, :]
        lanes = slice(b * LANES, (b + 1) * LANES)
        outs.append(cw[0:1, lanes] * work_ref[slot, b, SUBLANES - 2:SUBLANES - 2 + rows, :]
                    + cw[1:2, lanes] * work_ref[slot, b, SUBLANES - 1:SUBLANES - 1 + rows, :]
                    + cw[2:3, lanes] * work_ref[slot, b, SUBLANES:, :])
    return jnp.concatenate(outs, axis=1)


def _resident(shape):
    nd = len(shape)
    return pl.BlockSpec(shape, lambda *_: (0,) * nd, pipeline_mode=pl.Buffered(1))


def _resident_layer(shape, layer):
    nd = len(shape)
    return pl.BlockSpec((None,) + tuple(shape), lambda *_: (layer,) + (0,) * nd, pipeline_mode=pl.Buffered(1))


def _params(*sem):
    return pltpu.CompilerParams(dimension_semantics=sem, vmem_limit_bytes=VMEM_LIMIT)


def _ffn_kernel(x_ref, g_ref, wup_ref, cw_ref, wdn_ref, gf_ref, o_ref, carry_ref, work_ref, *,
                tiles_per_seq, n_chunks, final_norm, layer):
    slots = work_ref.shape[0]
    h = _rmsnorm(x_ref[...], g_ref[layer:layer + 1, :]).astype(BF16)

    @pl.when(pl.program_id(0) % tiles_per_seq == 0)
    def _():
        carry_ref[...] = jnp.zeros_like(carry_ref)

    def cols(j):
        return slice(j * FC, (j + 1) * FC), slice(D_FF + j * FC, D_FF + (j + 1) * FC)

    def up_proj(j):
        for part, cs in enumerate(cols(j)):
            _stage_rows(work_ref, j % slots, jnp.dot(h, wup_ref[:, cs].astype(BF16), preferred_element_type=F32),
                        part * FC // LANES)

    def down_proj(j):
        cw = jnp.concatenate([cw_ref[:, cs] for cs in cols(j)], axis=1)
        c = _causal_conv3_staged(work_ref, j % slots, carry_ref, j, cw)
        a, u = c[:, :FC], c[:, FC:]
        act = (a * jax.nn.sigmoid(a) * u).astype(BF16)
        base = x_ref if j == 0 else o_ref
        o_ref[...] = base[...] + jnp.dot(act, wdn_ref[j * FC:(j + 1) * FC, :].astype(BF16),
                                         preferred_element_type=F32)

    for j in range(slots - 1):
        up_proj(j)
    for j in range(n_chunks):
        if j + slots - 1 < n_chunks:
            up_proj(j + slots - 1)
        down_proj(j)

    if final_norm:
        o_ref[...] = _rmsnorm(o_ref[...], gf_ref[...])


def _conv_ffn(x, g, w_up, conv_w_t, w_down, layer, g_final, final_norm, seq):
    m = x.shape[0]
    nc = D_FF // FC
    kern = functools.partial(_ffn_kernel, tiles_per_seq=seq // TM, n_chunks=nc, final_norm=final_norm, layer=layer)
    return pl.pallas_call(
        kern,
        out_shape=jax.ShapeDtypeStruct((m, D_MODEL), F32),
        grid=(m // TM,),
        in_specs=[
            pl.BlockSpec((TM, D_MODEL), lambda i: (i, 0)),
            _resident(g.shape),
            _resident_layer((D_MODEL, 2 * D_FF), layer),
            _resident_layer((3, 2 * D_FF), layer),
            _resident_layer((D_FF, D_MODEL), layer),
            _resident((1, D_MODEL)),
        ],
        out_specs=pl.BlockSpec((TM, D_MODEL), lambda i: (i, 0)),
        scratch_shapes=[
            pltpu.VMEM((nc, 2 * FC // LANES, SUBLANES, LANES), F32),
            pltpu.VMEM((FFN_AHEAD + 1, 2 * FC // LANES, SUBLANES + TM, LANES), F32),
        ],
        compiler_params=_params("arbitrary"),
        name="conv_ffn",
    )(x, g, w_up, conv_w_t, w_down, g_final.reshape(1, D_MODEL))


def _sc_kernel(x_ref, g_ref, win_ref, cw_ref, wout_ref, o_ref, carry_ref, work_ref, b_ref, *,
               tiles_per_seq, n_chunks, g_row):
    slots = work_ref.shape[0]
    h = _rmsnorm(x_ref[...], g_ref[g_row:g_row + 1, :]).astype(BF16)

    @pl.when(pl.program_id(0) % tiles_per_seq == 0)
    def _():
        carry_ref[...] = jnp.zeros_like(carry_ref)

    def in_proj(j):
        b, c, u = (jnp.dot(h, win_ref[:, part * SC_WIDTH + j * FC:part * SC_WIDTH + (j + 1) * FC].astype(BF16),
                           preferred_element_type=F32) for part in range(3))
        b_ref[j % slots] = b
        _stage_rows(work_ref, j % slots, c * u)

    def out_proj(j):
        conv = _causal_conv3_staged(work_ref, j % slots, carry_ref, j, cw_ref[:, j * FC:(j + 1) * FC])
        y = (b_ref[j % slots] * conv).astype(BF16)
        base = x_ref if j == 0 else o_ref
        o_ref[...] = base[...] + jnp.dot(y, wout_ref[j * FC:(j + 1) * FC, :].astype(BF16),
                                         preferred_element_type=F32)

    for j in range(slots - 1):
        in_proj(j)
    for j in range(n_chunks):
        if j + slots - 1 < n_chunks:
            in_proj(j + slots - 1)
        out_proj(j)


def _short_conv(x, g, g_row, w_in, conv_w_t, w_out, seq):
    m = x.shape[0]
    nc = SC_WIDTH // FC
    kern = functools.partial(_sc_kernel, tiles_per_seq=seq // SC_TM, n_chunks=nc, g_row=g_row)
    return pl.pallas_call(
        kern,
        out_shape=jax.ShapeDtypeStruct((m, D_MODEL), F32),
        grid=(m // SC_TM,),
        in_specs=[
            pl.BlockSpec((SC_TM, D_MODEL), lambda i: (i, 0)),
            _resident(g.shape),
            _resident((D_MODEL, 3 * SC_WIDTH)),
            _resident((3, SC_WIDTH)),
            _resident((SC_WIDTH, D_MODEL)),
        ],
        out_specs=pl.BlockSpec((SC_TM, D_MODEL), lambda i: (i, 0)),
        scratch_shapes=[
            pltpu.VMEM((nc, FC // LANES, SUBLANES, LANES), F32),
            pltpu.VMEM((SC_AHEAD + 1, FC // LANES, SUBLANES + SC_TM, LANES), F32),
            pltpu.VMEM((SC_AHEAD + 1, SC_TM, FC), F32),
        ],
        compiler_params=_params("arbitrary"),
        name="short_conv",
    )(x, g, w_in, conv_w_t, w_out)


def _chunk_cumsum(x):
    rows, w = x.shape
    per = GLA_CHUNK // SUBLANES
    g = x.reshape(rows // SUBLANES, SUBLANES, w)
    sub = lax.broadcasted_iota(jnp.int32, g.shape, 1)
    shift = 1
    while shift < SUBLANES:
        g = g + jnp.where(sub >= shift, pltpu.roll(g, shift, 1), 0.0)
        shift *= 2
    g = g.reshape(rows // GLA_CHUNK, per, SUBLANES, w)
    tot = jnp.broadcast_to(g[:, :, SUBLANES - 1:SUBLANES, :], g.shape)
    parts = [g[:, 0]]
    off = tot[:, 0]
    for i in range(1, per):
        parts.append(g[:, i] + off)
        off = off + tot[:, i]
    return jnp.stack(parts, axis=1).reshape(rows, w)


def _gla_kernel(x_ref, g_ref, wint_ref, wgu_ref, bg_ref, hg_ref, wout_ref, o_ref,
                st_ref, q_ref, k_ref, qin_ref, kin_ref, kdec_ref, v_ref, gate_ref, y_ref, h_ref, z_ref, dec_ref,
                winb_ref, wab_ref, wgub_ref, woutb_ref, *, g_row):
    n_chunks = SB // GLA_CHUNK
    nt = (((1,), (1,)), ((), ()))
    tn = (((0,), (0,)), ((), ()))
    heads = range(GLA_HEADS)
    ks = [slice(hd * GLA_HEAD_K, (hd + 1) * GLA_HEAD_K) for hd in heads]
    vs = [slice(hd * GLA_HEAD_V, (hd + 1) * GLA_HEAD_V) for hd in heads]
    rows = [slice(n * GLA_CHUNK, (n + 1) * GLA_CHUNK) for n in range(n_chunks)]

    @pl.when(pl.program_id(1) == 0)
    def _():
        st_ref[...] = jnp.zeros_like(st_ref)

    @pl.when((pl.program_id(0) == 0) & (pl.program_id(1) == 0))
    def _():
        for c0 in range(0, winb_ref.shape[1], LANES):
            winb_ref[:, c0:c0 + LANES] = wint_ref[c0:c0 + LANES, :].T.astype(BF16)
        wa_t = jnp.concatenate([wint_ref[winb_ref.shape[1]:, :],
                                jnp.zeros((LANES - GLA_GATE_RANK, D_MODEL), F32)], axis=0)
        wab_ref[...] = wa_t.T.astype(BF16)
        wgub_ref[...] = jnp.concatenate([wgu_ref[...], jnp.zeros((LANES - GLA_GATE_RANK, GLA_KEY_DIM), F32)],
                                        axis=0).astype(BF16)
        woutb_ref[...] = wout_ref[...].astype(BF16)

    r = lax.broadcasted_iota(jnp.int32, (SB, SB), 0)
    c = lax.broadcasted_iota(jnp.int32, (SB, SB), 1)
    keep = (c <= r) & (c >= r - r % GLA_CHUNK)


    def front_a(sb):
        s = sb % 2
        h = _rmsnorm(x_ref[sb * SB:(sb + 1) * SB, :], g_ref[g_row:g_row + 1, :]).astype(BF16)
        h_ref[s] = h
        a_low = jnp.dot(h, wab_ref[...], preferred_element_type=F32).astype(BF16)
        q_ref[s] = jnp.dot(h, winb_ref[:, :GLA_KEY_DIM], preferred_element_type=F32) * (GLA_HEAD_K ** -0.5)
        k_ref[s] = jnp.dot(h, winb_ref[:, GLA_KEY_DIM:2 * GLA_KEY_DIM], preferred_element_type=F32)
        z_ref[s] = jnp.dot(a_low, wgub_ref[...], preferred_element_type=F32) + bg_ref[...]

    def front_b(sb):
        s = sb % 2
        h = h_ref[s]
        v_ref[s] = jnp.dot(h, winb_ref[:, 2 * GLA_KEY_DIM:2 * GLA_KEY_DIM + GLA_VAL_DIM],
                           preferred_element_type=F32).astype(BF16)
        gate_ref[s] = jnp.dot(h, winb_ref[:, 2 * GLA_KEY_DIM + GLA_VAL_DIM:], preferred_element_type=F32)
        z = z_ref[s]
        log_a = (jnp.minimum(z, 0.0) - jnp.log(1.0 + jnp.exp(-jnp.abs(z)))) * (1.0 / GLA_GATE_TAU)
        bcum = _chunk_cumsum(log_a).reshape(n_chunks, GLA_CHUNK, GLA_KEY_DIM)
        b_last = bcum[:, GLA_CHUNK - 1:, :]
        dec_ref[s] = jnp.broadcast_to(jnp.exp(b_last), dec_ref.shape[1:])
        to_rows = lambda t: t.reshape(SB, GLA_KEY_DIM)
        k_in = k_ref[s] * to_rows(jnp.exp(-bcum))
        kin_ref[s] = k_in.astype(BF16)
        decay_rows = jnp.broadcast_to(jnp.exp(b_last), bcum.shape)
        kdec_ref[s] = (k_in * to_rows(decay_rows)).astype(BF16)
        qin_ref[s] = (q_ref[s] * to_rows(jnp.exp(bcum))).astype(BF16)

    def back_a(sb):
        s = sb % 2
        kv_t = [[lax.dot_general(v_ref[s, rows[n], vs[hd]], kdec_ref[s, rows[n], ks[hd]], tn,
                                 preferred_element_type=F32) for hd in heads] for n in range(n_chunks)]
        scores = [lax.dot_general(qin_ref[s, :, ks[hd]], kin_ref[s, :, ks[hd]], nt, preferred_element_type=F32)
                  for hd in heads]
        probs = [jnp.where(keep, scores[hd], 0.0).astype(BF16) for hd in heads]
        o_intra = [jnp.dot(probs[hd], v_ref[s, :, vs[hd]], preferred_element_type=F32) for hd in heads]
        st = [st_ref[hd] for hd in heads]
        o_inter = [[None] * n_chunks for _ in heads]
        for n in range(n_chunks):
            for hd in heads:
                o_inter[hd][n] = lax.dot_general(qin_ref[s, rows[n], ks[hd]], st[hd].astype(BF16), nt,
                                                 preferred_element_type=F32)
            for hd in heads:
                st[hd] = dec_ref[s, n, 0:1, ks[hd]] * st[hd] + kv_t[n][hd]
        for hd in heads:
            st_ref[hd] = st[hd]
        for hd in heads:
            o = o_intra[hd] + jnp.concatenate(o_inter[hd], axis=0)
            on = o * lax.rsqrt(jnp.mean(o * o, axis=-1, keepdims=True) + RMS_EPS) * hg_ref[hd:hd + 1, :]
            gate = gate_ref[s, :, vs[hd]]
            y_ref[s, :, vs[hd]] = (on * (gate * jax.nn.sigmoid(gate))).astype(BF16)

    def back_b(sb):
        blk = slice(sb * SB, (sb + 1) * SB)
        o_ref[blk, :] = x_ref[blk, :] + jnp.dot(y_ref[sb % 2], woutb_ref[...], preferred_element_type=F32)

    n_sub = TB // SB
    front_a(0)
    front_b(0)
    for sb in range(n_sub):
        if sb + 1 < n_sub:
            front_a(sb + 1)
        back_a(sb)
        if sb + 1 < n_sub:
            front_b(sb + 1)
        back_b(sb)


def _gla_mixer(x, g, g_row, w_in, w_gate_up, b_gate, head_g, w_out, batch):
    m = x.shape[0]
    tiles = m // batch // TB
    tile = pl.BlockSpec((TB, D_MODEL), lambda b, t: (b * tiles + t, 0))
    return pl.pallas_call(
        functools.partial(_gla_kernel, g_row=g_row),
        out_shape=jax.ShapeDtypeStruct((m, D_MODEL), F32),
        grid=(batch, tiles),
        in_specs=[
            tile,
            _resident(g.shape),
            _resident(w_in.T.shape),
            _resident((GLA_GATE_RANK, GLA_KEY_DIM)),
            _resident((1, GLA_KEY_DIM)),
            _resident((GLA_HEADS, GLA_HEAD_V)),
            _resident((GLA_VAL_DIM, D_MODEL)),
        ],
        out_specs=tile,
        scratch_shapes=[
            pltpu.VMEM((GLA_HEADS, GLA_HEAD_V, GLA_HEAD_K), F32),
            pltpu.VMEM((2, SB, GLA_KEY_DIM), F32),
            pltpu.VMEM((2, SB, GLA_KEY_DIM), F32),
            pltpu.VMEM((2, SB, GLA_KEY_DIM), BF16),
            pltpu.VMEM((2, SB, GLA_KEY_DIM), BF16),
            pltpu.VMEM((2, SB, GLA_KEY_DIM), BF16),
            pltpu.VMEM((2, SB, GLA_VAL_DIM), BF16),
            pltpu.VMEM((2, SB, GLA_VAL_DIM), F32),
            pltpu.VMEM((2, SB, GLA_VAL_DIM), BF16),
            pltpu.VMEM((2, SB, D_MODEL), BF16),
            pltpu.VMEM((2, SB, GLA_KEY_DIM), F32),
            pltpu.VMEM((2, SB // GLA_CHUNK, SUBLANES, GLA_KEY_DIM), F32),
            pltpu.VMEM((D_MODEL, 2 * GLA_KEY_DIM + 2 * GLA_VAL_DIM), BF16),
            pltpu.VMEM((D_MODEL, LANES), BF16),
            pltpu.VMEM((LANES, GLA_KEY_DIM), BF16),
            pltpu.VMEM((GLA_VAL_DIM, D_MODEL), BF16),
        ],
        compiler_params=_params("arbitrary", "arbitrary"),
        name="gla_mixer",
    )(x, g, w_in.T, w_gate_up, b_gate, head_g, w_out)


def kernel(x, norm_mix_g, norm_ffn_g, gla_w_in, gla_w_gate_up, gla_b_gate, gla_head_norm_g, gla_w_out,
           sc_w_in, sc_conv_w, sc_w_out, ffn_w_up, ffn_conv_w, ffn_w_down, final_norm_g):
    batch, seq, d = x.shape
    assert d == D_MODEL and seq % TM == 0 and seq % SC_TM == 0 and seq % TB == 0
    xf = x.reshape(batch * seq, d)

    ffn_conv_w_t = jnp.swapaxes(ffn_conv_w, 1, 2)
    xf = _gla_mixer(xf, norm_mix_g, 0, gla_w_in[0], gla_w_gate_up[0], gla_b_gate, gla_head_norm_g[0],
                    gla_w_out[0], batch)
    xf = _conv_ffn(xf, norm_ffn_g, ffn_w_up, ffn_conv_w_t, ffn_w_down, 0, final_norm_g, False, seq)
    xf = _short_conv(xf, norm_mix_g, 1, sc_w_in[0], sc_conv_w[0].T, sc_w_out[0], seq)
    xf = _conv_ffn(xf, norm_ffn_g, ffn_w_up, ffn_conv_w_t, ffn_w_down, 1, final_norm_g, True, seq)
    return xf.reshape(batch, seq, d)
```

```python
import functools

import jax
import jax.numpy as jnp
from jax import lax
from jax.experimental import pallas as pl
from jax.experimental.pallas import tpu as pltpu

F32 = jnp.float32
BF16 = jnp.bfloat16

D_MODEL = 1024
GLA_HEADS = 4
GLA_KEY_DIM = D_MODEL // 2
GLA_VAL_DIM = D_MODEL
GLA_HEAD_K = GLA_KEY_DIM // GLA_HEADS
GLA_HEAD_V = GLA_VAL_DIM // GLA_HEADS
GLA_GATE_RANK = 16
GLA_GATE_TAU = 16.0
GLA_CHUNK = 64
SC_WIDTH = D_MODEL
D_FF = ((8 * D_MODEL // 3 + 255) // 256) * 256
RMS_EPS = 1e-6

SUBLANES = 8
LANES = 128
VMEM_LIMIT = 56 * 1024 * 1024

TM = 512
TB = 1024
SB = 256
SC_TM = 512
FC = 256
FFN_AHEAD = 2
SC_AHEAD = 1


def _rmsnorm(x, g):
    return x * lax.rsqrt(jnp.mean(x * x, axis=-1, keepdims=True) + RMS_EPS) * g


def _stage_rows(work_ref, slot, val, first=0):
    for b in range(val.shape[1] // LANES):
        work_ref[slot, first + b, SUBLANES:, :] = val[:, b * LANES:(b + 1) * LANES]


def _causal_conv3_staged(work_ref, slot, carry_ref, j, cw):
    rows = work_ref.shape[2] - SUBLANES
    outs = []
    for b in range(work_ref.shape[1]):
        work_ref[slot, b, :SUBLANES, :] = carry_ref[j, b]
        carry_ref[j, b] = work_ref[slot, b, rows:, :]
        lanes = slice(b * LANES, (b + 1) * LANES)
        outs.append(cw[0:1, lanes] * work_ref[slot, b, SUBLANES - 2:SUBLANES - 2 + rows, :]
                    + cw[1:2, lanes] * work_ref[slot, b, SUBLANES - 1:SUBLANES - 1 + rows, :]
                    + cw[2:3, lanes] * work_ref[slot, b, SUBLANES:, :])
    return jnp.concatenate(outs, axis=1)


def _resident(shape):
    nd = len(shape)
    return pl.BlockSpec(shape, lambda *_: (0,) * nd, pipeline_mode=pl.Buffered(1))


def _resident_layer(shape, layer):
    nd = len(shape)
    return pl.BlockSpec((None,) + tuple(shape), lambda *_: (layer,) + (0,) * nd, pipeline_mode=pl.Buffered(1))


def _params(*sem):
    return pltpu.CompilerParams(dimension_semantics=sem, vmem_limit_bytes=VMEM_LIMIT)


def _ffn_kernel(x_ref, g_ref, wup_ref, cw_ref, wdn_ref, gf_ref, o_ref, carry_ref, work_ref, *,
                tiles_per_seq, n_chunks, final_norm, layer):
    slots = work_ref.shape[0]
    h = _rmsnorm(x_ref[...], g_ref[layer:layer + 1, :]).astype(BF16)

    @pl.when(pl.program_id(0) % tiles_per_seq == 0)
    def _():
        carry_ref[...] = jnp.zeros_like(carry_ref)

    def cols(j):
        return slice(j * FC, (j + 1) * FC), slice(D_FF + j * FC, D_FF + (j + 1) * FC)

    def up_proj(j):
        for part, cs in enumerate(cols(j)):
            _stage_rows(work_ref, j % slots, jnp.dot(h, wup_ref[:, cs].astype(BF16), preferred_element_type=F32),
                        part * FC // LANES)

    def down_proj(j):
        cw = jnp.concatenate([cw_ref[:, cs] for cs in cols(j)], axis=1)
        c = _causal_conv3_staged(work_ref, j % slots, carry_ref, j, cw)
        a, u = c[:, :FC], c[:, FC:]
        act = (a * jax.nn.sigmoid(a) * u).astype(BF16)
        base = x_ref if j == 0 else o_ref
        w = wdn_ref[j * FC:(j + 1) * FC, :].astype(BF16)
        if j == n_chunks - 1:
            for rows in (slice(0, TM // 2), slice(TM // 2, TM)):
                out = base[rows, :] + jnp.dot(act[rows], w, preferred_element_type=F32)
                o_ref[rows, :] = _rmsnorm(out, gf_ref[...]) if final_norm else out
        else:
            o_ref[...] = base[...] + jnp.dot(act, w, preferred_element_type=F32)

    for j in range(slots - 1):
        up_proj(j)
    for j in range(n_chunks):
        if j + slots - 1 < n_chunks:
            up_proj(j + slots - 1)
        down_proj(j)


def _conv_ffn(x, g, w_up, conv_w_t, w_down, layer, g_final, final_norm, seq):
    m = x.shape[0]
    nc = D_FF // FC
    kern = functools.partial(_ffn_kernel, tiles_per_seq=seq // TM, n_chunks=nc, final_norm=final_norm, layer=layer)
    return pl.pallas_call(
        kern,
        out_shape=jax.ShapeDtypeStruct((m, D_MODEL), F32),
        grid=(m // TM,),
        in_specs=[
            pl.BlockSpec((TM, D_MODEL), lambda i: (i, 0)),
            _resident(g.shape),
            _resident_layer((D_MODEL, 2 * D_FF), layer),
            _resident_layer((3, 2 * D_FF), layer),
            _resident_layer((D_FF, D_MODEL), layer),
            _resident((1, D_MODEL)),
        ],
        out_specs=pl.BlockSpec((TM, D_MODEL), lambda i: (i, 0)),
        scratch_shapes=[
            pltpu.VMEM((nc, 2 * FC // LANES, SUBLANES, LANES), F32),
            pltpu.VMEM((FFN_AHEAD + 1, 2 * FC // LANES, SUBLANES + TM, LANES), F32),
        ],
        compiler_params=_params("arbitrary"),
        name="conv_ffn",
    )(x, g, w_up, conv_w_t, w_down, g_final.reshape(1, D_MODEL))


def _sc_kernel(x_ref, g_ref, win_ref, cw_ref, wout_ref, o_ref, carry_ref, work_ref, b_ref, *,
               tiles_per_seq, n_chunks, g_row):
    slots = work_ref.shape[0]
    h = _rmsnorm(x_ref[...], g_ref[g_row:g_row + 1, :]).astype(BF16)

    @pl.when(pl.program_id(0) % tiles_per_seq == 0)
    def _():
        carry_ref[...] = jnp.zeros_like(carry_ref)

    def in_proj(j):
        b, c, u = (jnp.dot(h, win_ref[:, part * SC_WIDTH + j * FC:part * SC_WIDTH + (j + 1) * FC].astype(BF16),
                           preferred_element_type=F32) for part in range(3))
        b_ref[j % slots] = b
        _stage_rows(work_ref, j % slots, c * u)

    def out_proj(j):
        conv = _causal_conv3_staged(work_ref, j % slots, carry_ref, j, cw_ref[:, j * FC:(j + 1) * FC])
        y = (b_ref[j % slots] * conv).astype(BF16)
        base = x_ref if j == 0 else o_ref
        o_ref[...] = base[...] + jnp.dot(y, wout_ref[j * FC:(j + 1) * FC, :].astype(BF16),
                                         preferred_element_type=F32)

    for j in range(slots - 1):
        in_proj(j)
    for j in range(n_chunks):
        if j + slots - 1 < n_chunks:
            in_proj(j + slots - 1)
        out_proj(j)


def _short_conv(x, g, g_row, w_in, conv_w_t, w_out, seq):
    m = x.shape[0]
    nc = SC_WIDTH // FC
    kern = functools.partial(_sc_kernel, tiles_per_seq=seq // SC_TM, n_chunks=nc, g_row=g_row)
    return pl.pallas_call(
        kern,
        out_shape=jax.ShapeDtypeStruct((m, D_MODEL), F32),
        grid=(m // SC_TM,),
        in_specs=[
            pl.BlockSpec((SC_TM, D_MODEL), lambda i: (i, 0)),
            _resident(g.shape),
            _resident((D_MODEL, 3 * SC_WIDTH)),
            _resident((3, SC_WIDTH)),
            _resident((SC_WIDTH, D_MODEL)),
        ],
        out_specs=pl.BlockSpec((SC_TM, D_MODEL), lambda i: (i, 0)),
        scratch_shapes=[
            pltpu.VMEM((nc, FC // LANES, SUBLANES, LANES), F32),
            pltpu.VMEM((SC_AHEAD + 1, FC // LANES, SUBLANES + SC_TM, LANES), F32),
            pltpu.VMEM((SC_AHEAD + 1, SC_TM, FC), F32),
        ],
        compiler_params=_params("arbitrary"),
        name="short_conv",
    )(x, g, w_in, conv_w_t, w_out)


def _chunk_cumsum(x):
    rows, w = x.shape
    per = GLA_CHUNK // SUBLANES
    g = x.reshape(rows // SUBLANES, SUBLANES, w)
    sub = lax.broadcasted_iota(jnp.int32, g.shape, 1)
    shift = 1
    while shift < SUBLANES:
        g = g + jnp.where(sub >= shift, pltpu.roll(g, shift, 1), 0.0)
        shift *= 2
    g = g.reshape(rows // GLA_CHUNK, per, SUBLANES, w)
    tot = jnp.broadcast_to(g[:, :, SUBLANES - 1:SUBLANES, :], g.shape)
    parts = [g[:, 0]]
    off = tot[:, 0]
    for i in range(1, per):
        parts.append(g[:, i] + off)
        off = off + tot[:, i]
    return jnp.stack(parts, axis=1).reshape(rows, w)


def _gla_kernel(x_ref, g_ref, wint_ref, wgu_ref, bg_ref, hg_ref, wout_ref, o_ref,
                st_ref, q_ref, k_ref, qin_ref, kin_ref, kdec_ref, v_ref, gate_ref, y_ref, h_ref, z_ref, dec_ref,
                winb_ref, wab_ref, wgub_ref, woutb_ref, *, g_row):
    n_chunks = SB // GLA_CHUNK
    nt = (((1,), (1,)), ((), ()))
    tn = (((0,), (0,)), ((), ()))
    heads = range(GLA_HEADS)
    ks = [slice(hd * GLA_HEAD_K, (hd + 1) * GLA_HEAD_K) for hd in heads]
    vs = [slice(hd * GLA_HEAD_V, (hd + 1) * GLA_HEAD_V) for hd in heads]
    rows = [slice(n * GLA_CHUNK, (n + 1) * GLA_CHUNK) for n in range(n_chunks)]

    @pl.when(pl.program_id(1) == 0)
    def _():
        st_ref[...] = jnp.zeros_like(st_ref)

    @pl.when((pl.program_id(0) == 0) & (pl.program_id(1) == 0))
    def _():
        for c0 in range(0, winb_ref.shape[1], LANES):
            winb_ref[:, c0:c0 + LANES] = wint_ref[c0:c0 + LANES, :].T.astype(BF16)
        wa_t = jnp.concatenate([wint_ref[winb_ref.shape[1]:, :],
                                jnp.zeros((LANES - GLA_GATE_RANK, D_MODEL), F32)], axis=0)
        wab_ref[...] = wa_t.T.astype(BF16)
        wgub_ref[...] = jnp.concatenate([wgu_ref[...], jnp.zeros((LANES - GLA_GATE_RANK, GLA_KEY_DIM), F32)],
                                        axis=0).astype(BF16)
        woutb_ref[...] = wout_ref[...].astype(BF16)

    r = lax.broadcasted_iota(jnp.int32, (SB, SB), 0)
    c = lax.broadcasted_iota(jnp.int32, (SB, SB), 1)
    keep = (c <= r) & (c >= r - r % GLA_CHUNK)


    def front_a(sb):
        s = sb % 2
        h = _rmsnorm(x_ref[sb * SB:(sb + 1) * SB, :], g_ref[g_row:g_row + 1, :]).astype(BF16)
        h_ref[s] = h
        a_low = jnp.dot(h, wab_ref[...], preferred_element_type=F32).astype(BF16)
        q_ref[s] = jnp.dot(h, winb_ref[:, :GLA_KEY_DIM], preferred_element_type=F32) * (GLA_HEAD_K ** -0.5)
        k_ref[s] = jnp.dot(h, winb_ref[:, GLA_KEY_DIM:2 * GLA_KEY_DIM], preferred_element_type=F32)
        z_ref[s] = jnp.dot(a_low, wgub_ref[...], preferred_element_type=F32) + bg_ref[...]

    def front_b(sb):
        s = sb % 2
        h = h_ref[s]
        v_ref[s] = jnp.dot(h, winb_ref[:, 2 * GLA_KEY_DIM:2 * GLA_KEY_DIM + GLA_VAL_DIM],
                           preferred_element_type=F32).astype(BF16)
        gate_ref[s] = jnp.dot(h, winb_ref[:, 2 * GLA_KEY_DIM + GLA_VAL_DIM:], preferred_element_type=F32)
        z = z_ref[s]
        log_a = (jnp.minimum(z, 0.0) - jnp.log(1.0 + jnp.exp(-jnp.abs(z)))) * (1.0 / GLA_GATE_TAU)
        bcum = _chunk_cumsum(log_a).reshape(n_chunks, GLA_CHUNK, GLA_KEY_DIM)
        b_last = bcum[:, GLA_CHUNK - 1:, :]
        dec_ref[s] = jnp.broadcast_to(jnp.exp(b_last), dec_ref.shape[1:])
        to_rows = lambda t: t.reshape(SB, GLA_KEY_DIM)
        k_in = k_ref[s] * to_rows(jnp.exp(-bcum))
        kin_ref[s] = k_in.astype(BF16)
        decay_rows = jnp.broadcast_to(jnp.exp(b_last), bcum.shape)
        kdec_ref[s] = (k_in * to_rows(decay_rows)).astype(BF16)
        qin_ref[s] = (q_ref[s] * to_rows(jnp.exp(bcum))).astype(BF16)

    def back_a(sb):
        s = sb % 2
        kv_t = [[lax.dot_general(v_ref[s, rows[n], vs[hd]], kdec_ref[s, rows[n], ks[hd]], tn,
                                 preferred_element_type=F32) for hd in heads] for n in range(n_chunks)]
        scores = [lax.dot_general(qin_ref[s, :, ks[hd]], kin_ref[s, :, ks[hd]], nt, preferred_element_type=F32)
                  for hd in heads]
        probs = [jnp.where(keep, scores[hd], 0.0).astype(BF16) for hd in heads]
        o_intra = [jnp.dot(probs[hd], v_ref[s, :, vs[hd]], preferred_element_type=F32) for hd in heads]
        st = [st_ref[hd] for hd in heads]
        o_inter = [[None] * n_chunks for _ in heads]
        for n in range(n_chunks):
            for hd in heads:
                o_inter[hd][n] = lax.dot_general(qin_ref[s, rows[n], ks[hd]], st[hd].astype(BF16), nt,
                                                 preferred_element_type=F32)
            for hd in heads:
                st[hd] = dec_ref[s, n, 0:1, ks[hd]] * st[hd] + kv_t[n][hd]
        for hd in heads:
            st_ref[hd] = st[hd]
        for hd in heads:
            o = o_intra[hd] + jnp.concatenate(o_inter[hd], axis=0)
            on = o * lax.rsqrt(jnp.mean(o * o, axis=-1, keepdims=True) + RMS_EPS) * hg_ref[hd:hd + 1, :]
            gate = gate_ref[s, :, vs[hd]]
            y_ref[s, :, vs[hd]] = (on * (gate * jax.nn.sigmoid(gate))).astype(BF16)

    def back_b(sb):
        blk = slice(sb * SB, (sb + 1) * SB)
        o_ref[blk, :] = x_ref[blk, :] + jnp.dot(y_ref[sb % 2], woutb_ref[...], preferred_element_type=F32)

    n_sub = TB // SB
    front_a(0)
    front_b(0)
    for sb in range(n_sub):
        if sb + 1 < n_sub:
            front_a(sb + 1)
        back_a(sb)
        if sb + 1 < n_sub:
            front_b(sb + 1)
        back_b(sb)


def _gla_mixer(x, g, g_row, w_in, w_gate_up, b_gate, head_g, w_out, batch):
    m = x.shape[0]
    tiles = m // batch // TB
    tile = pl.BlockSpec((TB, D_MODEL), lambda b, t: (b * tiles + t, 0))
    return pl.pallas_call(
        functools.partial(_gla_kernel, g_row=g_row),
        out_shape=jax.ShapeDtypeStruct((m, D_MODEL), F32),
        grid=(batch, tiles),
        in_specs=[
            tile,
            _resident(g.shape),
            _resident(w_in.T.shape),
            _resident((GLA_GATE_RANK, GLA_KEY_DIM)),
            _resident((1, GLA_KEY_DIM)),
            _resident((GLA_HEADS, GLA_HEAD_V)),
            _resident((GLA_VAL_DIM, D_MODEL)),
        ],
        out_specs=tile,
        scratch_shapes=[
            pltpu.VMEM((GLA_HEADS, GLA_HEAD_V, GLA_HEAD_K), F32),
            pltpu.VMEM((2, SB, GLA_KEY_DIM), F32),
            pltpu.VMEM((2, SB, GLA_KEY_DIM), F32),
            pltpu.VMEM((2, SB, GLA_KEY_DIM), BF16),
            pltpu.VMEM((2, SB, GLA_KEY_DIM), BF16),
            pltpu.VMEM((2, SB, GLA_KEY_DIM), BF16),
            pltpu.VMEM((2, SB, GLA_VAL_DIM), BF16),
            pltpu.VMEM((2, SB, GLA_VAL_DIM), F32),
            pltpu.VMEM((2, SB, GLA_VAL_DIM), BF16),
            pltpu.VMEM((2, SB, D_MODEL), BF16),
            pltpu.VMEM((2, SB, GLA_KEY_DIM), F32),
            pltpu.VMEM((2, SB // GLA_CHUNK, SUBLANES, GLA_KEY_DIM), F32),
            pltpu.VMEM((D_MODEL, 2 * GLA_KEY_DIM + 2 * GLA_VAL_DIM), BF16),
            pltpu.VMEM((D_MODEL, LANES), BF16),
            pltpu.VMEM((LANES, GLA_KEY_DIM), BF16),
            pltpu.VMEM((GLA_VAL_DIM, D_MODEL), BF16),
        ],
        compiler_params=_params("arbitrary", "arbitrary"),
        name="gla_mixer",
    )(x, g, w_in.T, w_gate_up, b_gate, head_g, w_out)


def kernel(x, norm_mix_g, norm_ffn_g, gla_w_in, gla_w_gate_up, gla_b_gate, gla_head_norm_g, gla_w_out,
           sc_w_in, sc_conv_w, sc_w_out, ffn_w_up, ffn_conv_w, ffn_w_down, final_norm_g):
    batch, seq, d = x.shape
    assert d == D_MODEL and seq % TM == 0 and seq % SC_TM == 0 and seq % TB == 0
    xf = x.reshape(batch * seq, d)

    ffn_conv_w_t = jnp.swapaxes(ffn_conv_w, 1, 2)
    xf = _gla_mixer(xf, norm_mix_g, 0, gla_w_in[0], gla_w_gate_up[0], gla_b_gate, gla_head_norm_g[0],
                    gla_w_out[0], batch)
    xf = _conv_ffn(xf, norm_ffn_g, ffn_w_up, ffn_conv_w_t, ffn_w_down, 0, final_norm_g, False, seq)
    xf = _short_conv(xf, norm_mix_g, 1, sc_w_in[0], sc_conv_w[0].T, sc_w_out[0], seq)
    xf = _conv_ffn(xf, norm_ffn_g, ffn_w_up, ffn_conv_w_t, ffn_w_down, 1, final_norm_g, True, seq)
    return xf.reshape(batch, seq, d)
```

```python
import functools

import jax
import jax.numpy as jnp
from jax import lax
from jax.experimental import pallas as pl
from jax.experimental.pallas import tpu as pltpu

F32 = jnp.float32
BF16 = jnp.bfloat16

D_MODEL = 1024
GLA_HEADS = 4
GLA_KEY_DIM = D_MODEL // 2
GLA_VAL_DIM = D_MODEL
GLA_HEAD_K = GLA_KEY_DIM // GLA_HEADS
GLA_HEAD_V = GLA_VAL_DIM // GLA_HEADS
GLA_GATE_RANK = 16
GLA_GATE_TAU = 16.0
GLA_CHUNK = 64
SC_WIDTH = D_MODEL
D_FF = ((8 * D_MODEL // 3 + 255) // 256) * 256
RMS_EPS = 1e-6

SUBLANES = 8
LANES = 128
VMEM_LIMIT = 56 * 1024 * 1024

TM = 512
TB = 1024
SB = 256
SC_TM = 512
FC = 256
FFN_AHEAD = 6
SC_AHEAD = 1


def _rmsnorm(x, g):
    return x * lax.rsqrt(jnp.mean(x * x, axis=-1, keepdims=True) + RMS_EPS) * g


def _stage_rows(work_ref, slot, val, first=0):
    for b in range(val.shape[1] // LANES):
        work_ref[slot, first + b, SUBLANES:, :] = val[:, b * LANES:(b + 1) * LANES]


def _causal_conv3_staged(work_ref, slot, carry_ref, j, cw):
    rows = work_ref.shape[2] - SUBLANES
    outs = []
    for b in range(work_ref.shape[1]):
        work_ref[slot, b, :SUBLANES, :] = carry_ref[j, b]
        carry_ref[j, b] = work_ref[slot, b, rows:, :]
        lanes = slice(b * LANES, (b + 1) * LANES)
        outs.append(cw[0:1, lanes] * work_ref[slot, b, SUBLANES - 2:SUBLANES - 2 + rows, :]
                    + cw[1:2, lanes] * work_ref[slot, b, SUBLANES - 1:SUBLANES - 1 + rows, :]
                    + cw[2:3, lanes] * work_ref[slot, b, SUBLANES:, :])
    return jnp.concatenate(outs, axis=1)


def _resident(shape):
    nd = len(shape)
    return pl.BlockSpec(shape, lambda *_: (0,) * nd, pipeline_mode=pl.Buffered(1))


def _resident_layer(shape, layer):
    nd = len(shape)
    return pl.BlockSpec((None,) + tuple(shape), lambda *_: (layer,) + (0,) * nd, pipeline_mode=pl.Buffered(1))


def _params(*sem):
    return pltpu.CompilerParams(dimension_semantics=sem, vmem_limit_bytes=VMEM_LIMIT)


def _ffn_kernel(x_ref, g_ref, wup_ref, cw_ref, wdn_ref, gf_ref, o_ref, carry_ref, work_ref, *,
                tiles_per_seq, n_chunks, final_norm, layer):
    slots = work_ref.shape[0]
    h = _rmsnorm(x_ref[...], g_ref[layer:layer + 1, :]).astype(BF16)

    @pl.when(pl.program_id(0) % tiles_per_seq == 0)
    def _():
        carry_ref[...] = jnp.zeros_like(carry_ref)

    def cols(j):
        return slice(j * FC, (j + 1) * FC), slice(D_FF + j * FC, D_FF + (j + 1) * FC)

    def up_proj(j):
        for part, cs in enumerate(cols(j)):
            _stage_rows(work_ref, j % slots, jnp.dot(h, wup_ref[:, cs].astype(BF16), preferred_element_type=F32),
                        part * FC // LANES)

    def activation(j):
        cw = jnp.concatenate([cw_ref[:, cs] for cs in cols(j)], axis=1)
        c = _causal_conv3_staged(work_ref, j % slots, carry_ref, j, cw)
        a, u = c[:, :FC], c[:, FC:]
        return (a * jax.nn.sigmoid(a) * u).astype(BF16)

    def down_proj(j, act):
        base = x_ref if j == 0 else o_ref
        w = wdn_ref[j * FC:(j + 1) * FC, :].astype(BF16)
        if j == n_chunks - 1:
            for rows in (slice(0, TM // 2), slice(TM // 2, TM)):
                out = base[rows, :] + jnp.dot(act[rows], w, preferred_element_type=F32)
                o_ref[rows, :] = _rmsnorm(out, gf_ref[...]) if final_norm else out
        else:
            o_ref[...] = base[...] + jnp.dot(act, w, preferred_element_type=F32)

    for j in range(slots - 1):
        up_proj(j)
    prev = None
    for j in range(n_chunks):
        if j + slots - 1 < n_chunks:
            up_proj(j + slots - 1)
        act = activation(j)
        if prev is not None:
            down_proj(j - 1, prev)
        prev = act
    down_proj(n_chunks - 1, prev)


def _conv_ffn(x, g, w_up, conv_w_t, w_down, layer, g_final, final_norm, seq):
    m = x.shape[0]
    nc = D_FF // FC
    kern = functools.partial(_ffn_kernel, tiles_per_seq=seq // TM, n_chunks=nc, final_norm=final_norm, layer=layer)
    return pl.pallas_call(
        kern,
        out_shape=jax.ShapeDtypeStruct((m, D_MODEL), F32),
        grid=(m // TM,),
        in_specs=[
            pl.BlockSpec((TM, D_MODEL), lambda i: (i, 0)),
            _resident(g.shape),
            _resident_layer((D_MODEL, 2 * D_FF), layer),
            _resident_layer((3, 2 * D_FF), layer),
            _resident_layer((D_FF, D_MODEL), layer),
            _resident((1, D_MODEL)),
        ],
        out_specs=pl.BlockSpec((TM, D_MODEL), lambda i: (i, 0)),
        scratch_shapes=[
            pltpu.VMEM((nc, 2 * FC // LANES, SUBLANES, LANES), F32),
            pltpu.VMEM((FFN_AHEAD + 1, 2 * FC // LANES, SUBLANES + TM, LANES), F32),
        ],
        compiler_params=_params("arbitrary"),
        name="conv_ffn",
    )(x, g, w_up, conv_w_t, w_down, g_final.reshape(1, D_MODEL))


def _sc_kernel(x_ref, g_ref, win_ref, cw_ref, wout_ref, o_ref, carry_ref, work_ref, b_ref, *,
               tiles_per_seq, n_chunks, g_row):
    slots = work_ref.shape[0]
    h = _rmsnorm(x_ref[...], g_ref[g_row:g_row + 1, :]).astype(BF16)

    @pl.when(pl.program_id(0) % tiles_per_seq == 0)
    def _():
        carry_ref[...] = jnp.zeros_like(carry_ref)

    def in_proj(j):
        b, c, u = (jnp.dot(h, win_ref[:, part * SC_WIDTH + j * FC:part * SC_WIDTH + (j + 1) * FC].astype(BF16),
                           preferred_element_type=F32) for part in range(3))
        b_ref[j % slots] = b
        _stage_rows(work_ref, j % slots, c * u)

    def out_proj(j):
        conv = _causal_conv3_staged(work_ref, j % slots, carry_ref, j, cw_ref[:, j * FC:(j + 1) * FC])
        y = (b_ref[j % slots] * conv).astype(BF16)
        base = x_ref if j == 0 else o_ref
        o_ref[...] = base[...] + jnp.dot(y, wout_ref[j * FC:(j + 1) * FC, :].astype(BF16),
                                         preferred_element_type=F32)

    for j in range(slots - 1):
        in_proj(j)
    for j in range(n_chunks):
        if j + slots - 1 < n_chunks:
            in_proj(j + slots - 1)
        out_proj(j)


def _short_conv(x, g, g_row, w_in, conv_w_t, w_out, seq):
    m = x.shape[0]
    nc = SC_WIDTH // FC
    kern = functools.partial(_sc_kernel, tiles_per_seq=seq // SC_TM, n_chunks=nc, g_row=g_row)
    return pl.pallas_call(
        kern,
        out_shape=jax.ShapeDtypeStruct((m, D_MODEL), F32),
        grid=(m // SC_TM,),
        in_specs=[
            pl.BlockSpec((SC_TM, D_MODEL), lambda i: (i, 0)),
            _resident(g.shape),
            _resident((D_MODEL, 3 * SC_WIDTH)),
            _resident((3, SC_WIDTH)),
            _resident((SC_WIDTH, D_MODEL)),
        ],
        out_specs=pl.BlockSpec((SC_TM, D_MODEL), lambda i: (i, 0)),
        scratch_shapes=[
            pltpu.VMEM((nc, FC // LANES, SUBLANES, LANES), F32),
            pltpu.VMEM((SC_AHEAD + 1, FC // LANES, SUBLANES + SC_TM, LANES), F32),
            pltpu.VMEM((SC_AHEAD + 1, SC_TM, FC), F32),
        ],
        compiler_params=_params("arbitrary"),
        name="short_conv",
    )(x, g, w_in, conv_w_t, w_out)


def _chunk_cumsum(x):
    rows, w = x.shape
    per = GLA_CHUNK // SUBLANES
    g = x.reshape(rows // SUBLANES, SUBLANES, w)
    sub = lax.broadcasted_iota(jnp.int32, g.shape, 1)
    shift = 1
    while shift < SUBLANES:
        g = g + jnp.where(sub >= shift, pltpu.roll(g, shift, 1), 0.0)
        shift *= 2
    g = g.reshape(rows // GLA_CHUNK, per, SUBLANES, w)
    tot = jnp.broadcast_to(g[:, :, SUBLANES - 1:SUBLANES, :], g.shape)
    parts = [g[:, 0]]
    off = tot[:, 0]
    for i in range(1, per):
        parts.append(g[:, i] + off)
        off = off + tot[:, i]
    return jnp.stack(parts, axis=1).reshape(rows, w)


def _gla_kernel(x_ref, g_ref, wint_ref, wgu_ref, bg_ref, hg_ref, wout_ref, o_ref,
                st_ref, q_ref, k_ref, qin_ref, kin_ref, kdec_ref, v_ref, gate_ref, y_ref, h_ref, z_ref, dec_ref,
                winb_ref, wab_ref, wgub_ref, woutb_ref, *, g_row):
    n_chunks = SB // GLA_CHUNK
    nt = (((1,), (1,)), ((), ()))
    tn = (((0,), (0,)), ((), ()))
    heads = range(GLA_HEADS)
    ks = [slice(hd * GLA_HEAD_K, (hd + 1) * GLA_HEAD_K) for hd in heads]
    vs = [slice(hd * GLA_HEAD_V, (hd + 1) * GLA_HEAD_V) for hd in heads]
    rows = [slice(n * GLA_CHUNK, (n + 1) * GLA_CHUNK) for n in range(n_chunks)]

    @pl.when(pl.program_id(1) == 0)
    def _():
        st_ref[...] = jnp.zeros_like(st_ref)

    @pl.when((pl.program_id(0) == 0) & (pl.program_id(1) == 0))
    def _():
        for c0 in range(0, winb_ref.shape[1], LANES):
            winb_ref[:, c0:c0 + LANES] = wint_ref[c0:c0 + LANES, :].T.astype(BF16)
        wa_t = jnp.concatenate([wint_ref[winb_ref.shape[1]:, :],
                                jnp.zeros((LANES - GLA_GATE_RANK, D_MODEL), F32)], axis=0)
        wab_ref[...] = wa_t.T.astype(BF16)
        wgub_ref[...] = jnp.concatenate([wgu_ref[...], jnp.zeros((LANES - GLA_GATE_RANK, GLA_KEY_DIM), F32)],
                                        axis=0).astype(BF16)
        woutb_ref[...] = wout_ref[...].astype(BF16)

    r = lax.broadcasted_iota(jnp.int32, (SB, SB), 0)
    c = lax.broadcasted_iota(jnp.int32, (SB, SB), 1)
    keep = (c <= r) & (c >= r - r % GLA_CHUNK)


    def front_a(sb):
        s = sb % 2
        h = _rmsnorm(x_ref[sb * SB:(sb + 1) * SB, :], g_ref[g_row:g_row + 1, :]).astype(BF16)
        h_ref[s] = h
        a_low = jnp.dot(h, wab_ref[...], preferred_element_type=F32).astype(BF16)
        q_ref[s] = jnp.dot(h, winb_ref[:, :GLA_KEY_DIM], preferred_element_type=F32) * (GLA_HEAD_K ** -0.5)
        k_ref[s] = jnp.dot(h, winb_ref[:, GLA_KEY_DIM:2 * GLA_KEY_DIM], preferred_element_type=F32)
        z_ref[s] = jnp.dot(a_low, wgub_ref[...], preferred_element_type=F32) + bg_ref[...]

    def front_b(sb):
        s = sb % 2
        h = h_ref[s]
        v_ref[s] = jnp.dot(h, winb_ref[:, 2 * GLA_KEY_DIM:2 * GLA_KEY_DIM + GLA_VAL_DIM],
                           preferred_element_type=F32).astype(BF16)
        gate_ref[s] = jnp.dot(h, winb_ref[:, 2 * GLA_KEY_DIM + GLA_VAL_DIM:], preferred_element_type=F32)
        z = z_ref[s]
        log_a = (jnp.minimum(z, 0.0) - jnp.log(1.0 + jnp.exp(-jnp.abs(z)))) * (1.0 / GLA_GATE_TAU)
        bcum = _chunk_cumsum(log_a).reshape(n_chunks, GLA_CHUNK, GLA_KEY_DIM)
        b_last = bcum[:, GLA_CHUNK - 1:, :]
        dec_ref[s] = jnp.broadcast_to(jnp.exp(b_last), dec_ref.shape[1:])
        to_rows = lambda t: t.reshape(SB, GLA_KEY_DIM)
        k_in = k_ref[s] * to_rows(jnp.exp(-bcum))
        kin_ref[s] = k_in.astype(BF16)
        decay_rows = jnp.broadcast_to(jnp.exp(b_last), bcum.shape)
        kdec_ref[s] = (k_in * to_rows(decay_rows)).astype(BF16)
        qin_ref[s] = (q_ref[s] * to_rows(jnp.exp(bcum))).astype(BF16)

    def back_a(sb):
        s = sb % 2
        kv_t = [[lax.dot_general(v_ref[s, rows[n], vs[hd]], kdec_ref[s, rows[n], ks[hd]], tn,
                                 preferred_element_type=F32) for hd in heads] for n in range(n_chunks)]
        scores = [lax.dot_general(qin_ref[s, :, ks[hd]], kin_ref[s, :, ks[hd]], nt, preferred_element_type=F32)
                  for hd in heads]
        probs = [jnp.where(keep, scores[hd], 0.0).astype(BF16) for hd in heads]
        o_intra = [jnp.dot(probs[hd], v_ref[s, :, vs[hd]], preferred_element_type=F32) for hd in heads]
        st = [st_ref[hd] for hd in heads]
        o_inter = [[None] * n_chunks for _ in heads]
        for n in range(n_chunks):
            for hd in heads:
                o_inter[hd][n] = lax.dot_general(qin_ref[s, rows[n], ks[hd]], st[hd].astype(BF16), nt,
                                                 preferred_element_type=F32)
            for hd in heads:
                st[hd] = dec_ref[s, n, 0:1, ks[hd]] * st[hd] + kv_t[n][hd]
        for hd in heads:
            st_ref[hd] = st[hd]
        for hd in heads:
            o = o_intra[hd] + jnp.concatenate(o_inter[hd], axis=0)
            on = o * lax.rsqrt(jnp.mean(o * o, axis=-1, keepdims=True) + RMS_EPS) * hg_ref[hd:hd + 1, :]
            gate = gate_ref[s, :, vs[hd]]
            y_ref[s, :, vs[hd]] = (on * (gate * jax.nn.sigmoid(gate))).astype(BF16)

    def back_b(sb):
        blk = slice(sb * SB, (sb + 1) * SB)
        o_ref[blk, :] = x_ref[blk, :] + jnp.dot(y_ref[sb % 2], woutb_ref[...], preferred_element_type=F32)

    n_sub = TB // SB
    front_a(0)
    front_b(0)
    for sb in range(n_sub):
        if sb + 1 < n_sub:
            front_a(sb + 1)
        back_a(sb)
        if sb + 1 < n_sub:
            front_b(sb + 1)
        back_b(sb)


def _gla_mixer(x, g, g_row, w_in, w_gate_up, b_gate, head_g, w_out, batch):
    m = x.shape[0]
    tiles = m // batch // TB
    tile = pl.BlockSpec((TB, D_MODEL), lambda b, t: (b * tiles + t, 0))
    return pl.pallas_call(
        functools.partial(_gla_kernel, g_row=g_row),
        out_shape=jax.ShapeDtypeStruct((m, D_MODEL), F32),
        grid=(batch, tiles),
        in_specs=[
            tile,
            _resident(g.shape),
            _resident(w_in.T.shape),
            _resident((GLA_GATE_RANK, GLA_KEY_DIM)),
            _resident((1, GLA_KEY_DIM)),
            _resident((GLA_HEADS, GLA_HEAD_V)),
            _resident((GLA_VAL_DIM, D_MODEL)),
        ],
        out_specs=tile,
        scratch_shapes=[
            pltpu.VMEM((GLA_HEADS, GLA_HEAD_V, GLA_HEAD_K), F32),
            pltpu.VMEM((2, SB, GLA_KEY_DIM), F32),
            pltpu.VMEM((2, SB, GLA_KEY_DIM), F32),
            pltpu.VMEM((2, SB, GLA_KEY_DIM), BF16),
            pltpu.VMEM((2, SB, GLA_KEY_DIM), BF16),
            pltpu.VMEM((2, SB, GLA_KEY_DIM), BF16),
            pltpu.VMEM((2, SB, GLA_VAL_DIM), BF16),
            pltpu.VMEM((2, SB, GLA_VAL_DIM), F32),
            pltpu.VMEM((2, SB, GLA_VAL_DIM), BF16),
            pltpu.VMEM((2, SB, D_MODEL), BF16),
            pltpu.VMEM((2, SB, GLA_KEY_DIM), F32),
            pltpu.VMEM((2, SB // GLA_CHUNK, SUBLANES, GLA_KEY_DIM), F32),
            pltpu.VMEM((D_MODEL, 2 * GLA_KEY_DIM + 2 * GLA_VAL_DIM), BF16),
            pltpu.VMEM((D_MODEL, LANES), BF16),
            pltpu.VMEM((LANES, GLA_KEY_DIM), BF16),
            pltpu.VMEM((GLA_VAL_DIM, D_MODEL), BF16),
        ],
        compiler_params=_params("arbitrary", "arbitrary"),
        name="gla_mixer",
    )(x, g, w_in.T, w_gate_up, b_gate, head_g, w_out)


def kernel(x, norm_mix_g, norm_ffn_g, gla_w_in, gla_w_gate_up, gla_b_gate, gla_head_norm_g, gla_w_out,
           sc_w_in, sc_conv_w, sc_w_out, ffn_w_up, ffn_conv_w, ffn_w_down, final_norm_g):
    batch, seq, d = x.shape
    assert d == D_MODEL and seq % TM == 0 and seq % SC_TM == 0 and seq % TB == 0
    xf = x.reshape(batch * seq, d)

    ffn_conv_w_t = jnp.swapaxes(ffn_conv_w, 1, 2)
    xf = _gla_mixer(xf, norm_mix_g, 0, gla_w_in[0], gla_w_gate_up[0], gla_b_gate, gla_head_norm_g[0],
                    gla_w_out[0], batch)
    xf = _conv_ffn(xf, norm_ffn_g, ffn_w_up, ffn_conv_w_t, ffn_w_down, 0, final_norm_g, False, seq)
    xf = _short_conv(xf, norm_mix_g, 1, sc_w_in[0], sc_conv_w[0].T, sc_w_out[0], seq)
    xf = _conv_ffn(xf, norm_ffn_g, ffn_w_up, ffn_conv_w_t, ffn_w_down, 1, final_norm_g, True, seq)
    return xf.reshape(batch, seq, d)
```

```python
import functools

import jax
import jax.numpy as jnp
from jax import lax
from jax.experimental import pallas as pl
from jax.experimental.pallas import tpu as pltpu

F32 = jnp.float32
BF16 = jnp.bfloat16

D_MODEL = 1024
GLA_HEADS = 4
GLA_KEY_DIM = D_MODEL // 2
GLA_VAL_DIM = D_MODEL
GLA_HEAD_K = GLA_KEY_DIM // GLA_HEADS
GLA_HEAD_V = GLA_VAL_DIM // GLA_HEADS
GLA_GATE_RANK = 16
GLA_GATE_TAU = 16.0
GLA_CHUNK = 64
SC_WIDTH = D_MODEL
D_FF = ((8 * D_MODEL // 3 + 255) // 256) * 256
RMS_EPS = 1e-6

SUBLANES = 8
LANES = 128
VMEM_LIMIT = 56 * 1024 * 1024

TM = 512
TB = 1024
SB = 256
SC_TM = 512
FC = 256
FFN_AHEAD = 8
SC_AHEAD = 1


def _rmsnorm(x, g):
    return x * lax.rsqrt(jnp.mean(x * x, axis=-1, keepdims=True) + RMS_EPS) * g


def _stage_rows(work_ref, slot, val, first=0):
    for b in range(val.shape[1] // LANES):
        work_ref[slot, first + b, SUBLANES:, :] = val[:, b * LANES:(b + 1) * LANES]


def _causal_conv3_staged(work_ref, slot, carry_ref, j, cw):
    rows = work_ref.shape[2] - SUBLANES
    outs = []
    for b in range(work_ref.shape[1]):
        work_ref[slot, b, :SUBLANES, :] = carry_ref[j, b]
        carry_ref[j, b] = work_ref[slot, b, rows:, :]
        lanes = slice(b * LANES, (b + 1) * LANES)
        outs.append(cw[0:1, lanes] * work_ref[slot, b, SUBLANES - 2:SUBLANES - 2 + rows, :]
                    + cw[1:2, lanes] * work_ref[slot, b, SUBLANES - 1:SUBLANES - 1 + rows, :]
                    + cw[2:3, lanes] * work_ref[slot, b, SUBLANES:, :])
    return jnp.concatenate(outs, axis=1)


def _resident(shape):
    nd = len(shape)
    return pl.BlockSpec(shape, lambda *_: (0,) * nd, pipeline_mode=pl.Buffered(1))


def _resident_layer(shape, layer):
    nd = len(shape)
    return pl.BlockSpec((None,) + tuple(shape), lambda *_: (layer,) + (0,) * nd, pipeline_mode=pl.Buffered(1))


def _params(*sem):
    return pltpu.CompilerParams(dimension_semantics=sem, vmem_limit_bytes=VMEM_LIMIT)


def _ffn_kernel(x_ref, g_ref, wup_ref, cw_ref, wdn_ref, gf_ref, o_ref, carry_ref, work_ref, *,
                tiles_per_seq, n_chunks, final_norm, layer):
    slots = work_ref.shape[0]
    h = _rmsnorm(x_ref[...], g_ref[layer:layer + 1, :]).astype(BF16)

    @pl.when(pl.program_id(0) % tiles_per_seq == 0)
    def _():
        carry_ref[...] = jnp.zeros_like(carry_ref)

    def cols(j):
        return slice(j * FC, (j + 1) * FC), slice(D_FF + j * FC, D_FF + (j + 1) * FC)

    def up_proj(j):
        for part, cs in enumerate(cols(j)):
            _stage_rows(work_ref, j % slots, jnp.dot(h, wup_ref[:, cs].astype(BF16), preferred_element_type=F32),
                        part * FC // LANES)

    def activation(j):
        cw = jnp.concatenate([cw_ref[:, cs] for cs in cols(j)], axis=1)
        c = _causal_conv3_staged(work_ref, j % slots, carry_ref, j, cw)
        a, u = c[:, :FC], c[:, FC:]
        return (a * jax.nn.sigmoid(a) * u).astype(BF16)

    def down_proj(j, act):
        base = x_ref if j == 0 else o_ref
        w = wdn_ref[j * FC:(j + 1) * FC, :].astype(BF16)
        if j == n_chunks - 1:
            for rows in (slice(0, TM // 2), slice(TM // 2, TM)):
                out = base[rows, :] + jnp.dot(act[rows], w, preferred_element_type=F32)
                o_ref[rows, :] = _rmsnorm(out, gf_ref[...]) if final_norm else out
        else:
            o_ref[...] = base[...] + jnp.dot(act, w, preferred_element_type=F32)

    for j in range(slots - 1):
        up_proj(j)
    prev = None
    for j in range(n_chunks):
        if j + slots - 1 < n_chunks:
            up_proj(j + slots - 1)
        act = activation(j)
        if prev is not None:
            down_proj(j - 1, prev)
        prev = act
    down_proj(n_chunks - 1, prev)


def _conv_ffn(x, g, w_up, conv_w_t, w_down, layer, g_final, final_norm, seq):
    m = x.shape[0]
    nc = D_FF // FC
    kern = functools.partial(_ffn_kernel, tiles_per_seq=seq // TM, n_chunks=nc, final_norm=final_norm, layer=layer)
    return pl.pallas_call(
        kern,
        out_shape=jax.ShapeDtypeStruct((m, D_MODEL), F32),
        grid=(m // TM,),
        in_specs=[
            pl.BlockSpec((TM, D_MODEL), lambda i: (i, 0)),
            _resident(g.shape),
            _resident_layer((D_MODEL, 2 * D_FF), layer),
            _resident_layer((3, 2 * D_FF), layer),
            _resident_layer((D_FF, D_MODEL), layer),
            _resident((1, D_MODEL)),
        ],
        out_specs=pl.BlockSpec((TM, D_MODEL), lambda i: (i, 0)),
        scratch_shapes=[
            pltpu.VMEM((nc, 2 * FC // LANES, SUBLANES, LANES), F32),
            pltpu.VMEM((FFN_AHEAD + 1, 2 * FC // LANES, SUBLANES + TM, LANES), F32),
        ],
        compiler_params=_params("arbitrary"),
        name="conv_ffn",
    )(x, g, w_up, conv_w_t, w_down, g_final.reshape(1, D_MODEL))


def _sc_kernel(x_ref, g_ref, win_ref, cw_ref, wout_ref, o_ref, carry_ref, work_ref, b_ref, *,
               tiles_per_seq, n_chunks, g_row):
    slots = work_ref.shape[0]
    h = _rmsnorm(x_ref[...], g_ref[g_row:g_row + 1, :]).astype(BF16)

    @pl.when(pl.program_id(0) % tiles_per_seq == 0)
    def _():
        carry_ref[...] = jnp.zeros_like(carry_ref)

    def in_proj(j):
        b, c, u = (jnp.dot(h, win_ref[:, part * SC_WIDTH + j * FC:part * SC_WIDTH + (j + 1) * FC].astype(BF16),
                           preferred_element_type=F32) for part in range(3))
        b_ref[j % slots] = b
        _stage_rows(work_ref, j % slots, c * u)

    def out_proj(j):
        conv = _causal_conv3_staged(work_ref, j % slots, carry_ref, j, cw_ref[:, j * FC:(j + 1) * FC])
        y = (b_ref[j % slots] * conv).astype(BF16)
        base = x_ref if j == 0 else o_ref
        o_ref[...] = base[...] + jnp.dot(y, wout_ref[j * FC:(j + 1) * FC, :].astype(BF16),
                                         preferred_element_type=F32)

    for j in range(slots - 1):
        in_proj(j)
    for j in range(n_chunks):
        if j + slots - 1 < n_chunks:
            in_proj(j + slots - 1)
        out_proj(j)


def _short_conv(x, g, g_row, w_in, conv_w_t, w_out, seq):
    m = x.shape[0]
    nc = SC_WIDTH // FC
    kern = functools.partial(_sc_kernel, tiles_per_seq=seq // SC_TM, n_chunks=nc, g_row=g_row)
    return pl.pallas_call(
        kern,
        out_shape=jax.ShapeDtypeStruct((m, D_MODEL), F32),
        grid=(m // SC_TM,),
        in_specs=[
            pl.BlockSpec((SC_TM, D_MODEL), lambda i: (i, 0)),
            _resident(g.shape),
            _resident((D_MODEL, 3 * SC_WIDTH)),
            _resident((3, SC_WIDTH)),
            _resident((SC_WIDTH, D_MODEL)),
        ],
        out_specs=pl.BlockSpec((SC_TM, D_MODEL), lambda i: (i, 0)),
        scratch_shapes=[
            pltpu.VMEM((nc, FC // LANES, SUBLANES, LANES), F32),
            pltpu.VMEM((SC_AHEAD + 1, FC // LANES, SUBLANES + SC_TM, LANES), F32),
            pltpu.VMEM((SC_AHEAD + 1, SC_TM, FC), F32),
        ],
        compiler_params=_params("arbitrary"),
        name="short_conv",
    )(x, g, w_in, conv_w_t, w_out)


def _chunk_cumsum(x):
    rows, w = x.shape
    per = GLA_CHUNK // SUBLANES
    g = x.reshape(rows // SUBLANES, SUBLANES, w)
    sub = lax.broadcasted_iota(jnp.int32, g.shape, 1)
    shift = 1
    while shift < SUBLANES:
        g = g + jnp.where(sub >= shift, pltpu.roll(g, shift, 1), 0.0)
        shift *= 2
    g = g.reshape(rows // GLA_CHUNK, per, SUBLANES, w)
    tot = jnp.broadcast_to(g[:, :, SUBLANES - 1:SUBLANES, :], g.shape)
    parts = [g[:, 0]]
    off = tot[:, 0]
    for i in range(1, per):
        parts.append(g[:, i] + off)
        off = off + tot[:, i]
    return jnp.stack(parts, axis=1).reshape(rows, w)


def _gla_kernel(x_ref, g_ref, wint_ref, wgu_ref, bg_ref, hg_ref, wout_ref, o_ref,
                st_ref, q_ref, k_ref, qin_ref, kin_ref, kdec_ref, v_ref, gate_ref, y_ref, h_ref, z_ref, dec_ref,
                winb_ref, wab_ref, wgub_ref, woutb_ref, *, g_row):
    n_chunks = SB // GLA_CHUNK
    nt = (((1,), (1,)), ((), ()))
    tn = (((0,), (0,)), ((), ()))
    heads = range(GLA_HEADS)
    ks = [slice(hd * GLA_HEAD_K, (hd + 1) * GLA_HEAD_K) for hd in heads]
    vs = [slice(hd * GLA_HEAD_V, (hd + 1) * GLA_HEAD_V) for hd in heads]
    rows = [slice(n * GLA_CHUNK, (n + 1) * GLA_CHUNK) for n in range(n_chunks)]

    @pl.when(pl.program_id(1) == 0)
    def _():
        st_ref[...] = jnp.zeros_like(st_ref)

    @pl.when((pl.program_id(0) == 0) & (pl.program_id(1) == 0))
    def _():
        for c0 in range(0, winb_ref.shape[1], LANES):
            winb_ref[:, c0:c0 + LANES] = wint_ref[c0:c0 + LANES, :].T.astype(BF16)
        wa_t = jnp.concatenate([wint_ref[winb_ref.shape[1]:, :],
                                jnp.zeros((LANES - GLA_GATE_RANK, D_MODEL), F32)], axis=0)
        wab_ref[...] = wa_t.T.astype(BF16)
        wgub_ref[...] = jnp.concatenate([wgu_ref[...], jnp.zeros((LANES - GLA_GATE_RANK, GLA_KEY_DIM), F32)],
                                        axis=0).astype(BF16)
        woutb_ref[...] = wout_ref[...].astype(BF16)

    r = lax.broadcasted_iota(jnp.int32, (SB, SB), 0)
    c = lax.broadcasted_iota(jnp.int32, (SB, SB), 1)
    keep = (c <= r) & (c >= r - r % GLA_CHUNK)


    def front_a(sb):
        s = sb % 2
        h = _rmsnorm(x_ref[sb * SB:(sb + 1) * SB, :], g_ref[g_row:g_row + 1, :]).astype(BF16)
        h_ref[s] = h
        a_low = jnp.dot(h, wab_ref[...], preferred_element_type=F32).astype(BF16)
        q_ref[s] = jnp.dot(h, winb_ref[:, :GLA_KEY_DIM], preferred_element_type=F32) * (GLA_HEAD_K ** -0.5)
        k_ref[s] = jnp.dot(h, winb_ref[:, GLA_KEY_DIM:2 * GLA_KEY_DIM], preferred_element_type=F32)
        z_ref[s] = jnp.dot(a_low, wgub_ref[...], preferred_element_type=F32) + bg_ref[...]

    def front_b(sb):
        s = sb % 2
        h = h_ref[s]
        v_ref[s] = jnp.dot(h, winb_ref[:, 2 * GLA_KEY_DIM:2 * GLA_KEY_DIM + GLA_VAL_DIM],
                           preferred_element_type=F32).astype(BF16)
        gate_ref[s] = jnp.dot(h, winb_ref[:, 2 * GLA_KEY_DIM + GLA_VAL_DIM:], preferred_element_type=F32)
        z = z_ref[s]
        log_a = (jnp.minimum(z, 0.0) - jnp.log(1.0 + jnp.exp(-jnp.abs(z)))) * (1.0 / GLA_GATE_TAU)
        bcum = _chunk_cumsum(log_a).reshape(n_chunks, GLA_CHUNK, GLA_KEY_DIM)
        b_last = bcum[:, GLA_CHUNK - 1:, :]
        dec_ref[s] = jnp.broadcast_to(jnp.exp(b_last), dec_ref.shape[1:])
        to_rows = lambda t: t.reshape(SB, GLA_KEY_DIM)
        k_in = k_ref[s] * to_rows(jnp.exp(-bcum))
        kin_ref[s] = k_in.astype(BF16)
        decay_rows = jnp.broadcast_to(jnp.exp(b_last), bcum.shape)
        kdec_ref[s] = (k_in * to_rows(decay_rows)).astype(BF16)
        qin_ref[s] = (q_ref[s] * to_rows(jnp.exp(bcum))).astype(BF16)

    def back_a(sb):
        s = sb % 2
        kv_t = [[lax.dot_general(v_ref[s, rows[n], vs[hd]], kdec_ref[s, rows[n], ks[hd]], tn,
                                 preferred_element_type=F32) for hd in heads] for n in range(n_chunks)]
        scores = [lax.dot_general(qin_ref[s, :, ks[hd]], kin_ref[s, :, ks[hd]], nt, preferred_element_type=F32)
                  for hd in heads]
        probs = [jnp.where(keep, scores[hd], 0.0).astype(BF16) for hd in heads]
        o_intra = [jnp.dot(probs[hd], v_ref[s, :, vs[hd]], preferred_element_type=F32) for hd in heads]
        st = [st_ref[hd] for hd in heads]
        o_inter = [[None] * n_chunks for _ in heads]
        for n in range(n_chunks):
            for hd in heads:
                o_inter[hd][n] = lax.dot_general(qin_ref[s, rows[n], ks[hd]], st[hd].astype(BF16), nt,
                                                 preferred_element_type=F32)
            for hd in heads:
                st[hd] = dec_ref[s, n, 0:1, ks[hd]] * st[hd] + kv_t[n][hd]
        for hd in heads:
            st_ref[hd] = st[hd]
        for hd in heads:
            o = o_intra[hd] + jnp.concatenate(o_inter[hd], axis=0)
            on = o * lax.rsqrt(jnp.mean(o * o, axis=-1, keepdims=True) + RMS_EPS) * hg_ref[hd:hd + 1, :]
            gate = gate_ref[s, :, vs[hd]]
            y_ref[s, :, vs[hd]] = (on * (gate * jax.nn.sigmoid(gate))).astype(BF16)

    def back_b(sb):
        blk = slice(sb * SB, (sb + 1) * SB)
        o_ref[blk, :] = x_ref[blk, :] + jnp.dot(y_ref[sb % 2], woutb_ref[...], preferred_element_type=F32)

    n_sub = TB // SB
    front_a(0)
    front_b(0)
    for sb in range(n_sub):
        if sb + 1 < n_sub:
            front_a(sb + 1)
        back_a(sb)
        if sb + 1 < n_sub:
            front_b(sb + 1)
        back_b(sb)


def _gla_mixer(x, g, g_row, w_in, w_gate_up, b_gate, head_g, w_out, batch):
    m = x.shape[0]
    tiles = m // batch // TB
    tile = pl.BlockSpec((TB, D_MODEL), lambda b, t: (b * tiles + t, 0))
    return pl.pallas_call(
        functools.partial(_gla_kernel, g_row=g_row),
        out_shape=jax.ShapeDtypeStruct((m, D_MODEL), F32),
        grid=(batch, tiles),
        in_specs=[
            tile,
            _resident(g.shape),
            _resident(w_in.T.shape),
            _resident((GLA_GATE_RANK, GLA_KEY_DIM)),
            _resident((1, GLA_KEY_DIM)),
            _resident((GLA_HEADS, GLA_HEAD_V)),
            _resident((GLA_VAL_DIM, D_MODEL)),
        ],
        out_specs=tile,
        scratch_shapes=[
            pltpu.VMEM((GLA_HEADS, GLA_HEAD_V, GLA_HEAD_K), F32),
            pltpu.VMEM((2, SB, GLA_KEY_DIM), F32),
            pltpu.VMEM((2, SB, GLA_KEY_DIM), F32),
            pltpu.VMEM((2, SB, GLA_KEY_DIM), BF16),
            pltpu.VMEM((2, SB, GLA_KEY_DIM), BF16),
            pltpu.VMEM((2, SB, GLA_KEY_DIM), BF16),
            pltpu.VMEM((2, SB, GLA_VAL_DIM), BF16),
            pltpu.VMEM((2, SB, GLA_VAL_DIM), F32),
            pltpu.VMEM((2, SB, GLA_VAL_DIM), BF16),
            pltpu.VMEM((2, SB, D_MODEL), BF16),
            pltpu.VMEM((2, SB, GLA_KEY_DIM), F32),
            pltpu.VMEM((2, SB // GLA_CHUNK, SUBLANES, GLA_KEY_DIM), F32),
            pltpu.VMEM((D_MODEL, 2 * GLA_KEY_DIM + 2 * GLA_VAL_DIM), BF16),
            pltpu.VMEM((D_MODEL, LANES), BF16),
            pltpu.VMEM((LANES, GLA_KEY_DIM), BF16),
            pltpu.VMEM((GLA_VAL_DIM, D_MODEL), BF16),
        ],
        compiler_params=_params("arbitrary", "arbitrary"),
        name="gla_mixer",
    )(x, g, w_in.T, w_gate_up, b_gate, head_g, w_out)


def kernel(x, norm_mix_g, norm_ffn_g, gla_w_in, gla_w_gate_up, gla_b_gate, gla_head_norm_g, gla_w_out,
           sc_w_in, sc_conv_w, sc_w_out, ffn_w_up, ffn_conv_w, ffn_w_down, final_norm_g):
    batch, seq, d = x.shape
    assert d == D_MODEL and seq % TM == 0 and seq % SC_TM == 0 and seq % TB == 0
    xf = x.reshape(batch * seq, d)

    ffn_conv_w_t = jnp.swapaxes(ffn_conv_w, 1, 2)
    xf = _gla_mixer(xf, norm_mix_g, 0, gla_w_in[0], gla_w_gate_up[0], gla_b_gate, gla_head_norm_g[0],
                    gla_w_out[0], batch)
    xf = _conv_ffn(xf, norm_ffn_g, ffn_w_up, ffn_conv_w_t, ffn_w_down, 0, final_norm_g, False, seq)
    xf = _short_conv(xf, norm_mix_g, 1, sc_w_in[0], sc_conv_w[0].T, sc_w_out[0], seq)
    xf = _conv_ffn(xf, norm_ffn_g, ffn_w_up, ffn_conv_w_t, ffn_w_down, 1, final_norm_g, True, seq)
    return xf.reshape(batch, seq, d)
```

```python
import functools

import jax
import jax.numpy as jnp
from jax import lax
from jax.experimental import pallas as pl
from jax.experimental.pallas import tpu as pltpu

F32 = jnp.float32
BF16 = jnp.bfloat16

D_MODEL = 1024
GLA_HEADS = 4
GLA_KEY_DIM = D_MODEL // 2
GLA_VAL_DIM = D_MODEL
GLA_HEAD_K = GLA_KEY_DIM // GLA_HEADS
GLA_HEAD_V = GLA_VAL_DIM // GLA_HEADS
GLA_GATE_RANK = 16
GLA_GATE_TAU = 16.0
GLA_CHUNK = 64
SC_WIDTH = D_MODEL
D_FF = ((8 * D_MODEL // 3 + 255) // 256) * 256
RMS_EPS = 1e-6

SUBLANES = 8
LANES = 128
VMEM_LIMIT = 56 * 1024 * 1024

TM = 512
TB = 1024
SB = 256
SC_TM = 512
FC = 256
FFN_AHEAD = 8
SC_AHEAD = 1


def _rmsnorm(x, g):
    return x * lax.rsqrt(jnp.mean(x * x, axis=-1, keepdims=True) + RMS_EPS) * g


def _stage_rows(work_ref, slot, val, first=0):
    for b in range(val.shape[1] // LANES):
        work_ref[slot, first + b, SUBLANES:, :] = val[:, b * LANES:(b + 1) * LANES]


def _causal_conv3_staged(work_ref, slot, carry_ref, j, cw):
    rows = work_ref.shape[2] - SUBLANES
    outs = []
    for b in range(work_ref.shape[1]):
        work_ref[slot, b, :SUBLANES, :] = carry_ref[j, b]
        carry_ref[j, b] = work_ref[slot, b, rows:, :]
        lanes = slice(b * LANES, (b + 1) * LANES)
        outs.append(cw[0:1, lanes] * work_ref[slot, b, SUBLANES - 2:SUBLANES - 2 + rows, :]
                    + cw[1:2, lanes] * work_ref[slot, b, SUBLANES - 1:SUBLANES - 1 + rows, :]
                    + cw[2:3, lanes] * work_ref[slot, b, SUBLANES:, :])
    return jnp.concatenate(outs, axis=1)


def _resident(shape):
    nd = len(shape)
    return pl.BlockSpec(shape, lambda *_: (0,) * nd, pipeline_mode=pl.Buffered(1))


def _resident_layer(shape, layer):
    nd = len(shape)
    return pl.BlockSpec((None,) + tuple(shape), lambda *_: (layer,) + (0,) * nd, pipeline_mode=pl.Buffered(1))


def _params(*sem):
    return pltpu.CompilerParams(dimension_semantics=sem, vmem_limit_bytes=VMEM_LIMIT)


def _ffn_kernel(x_ref, g_ref, wup_ref, cw_ref, wdn_ref, gf_ref, o_ref, carry_ref, work_ref, *,
                tiles_per_seq, n_chunks, final_norm, layer):
    slots = work_ref.shape[0]
    h = _rmsnorm(x_ref[...], g_ref[layer:layer + 1, :]).astype(BF16)

    @pl.when(pl.program_id(0) % tiles_per_seq == 0)
    def _():
        carry_ref[...] = jnp.zeros_like(carry_ref)

    def cols(j):
        return slice(j * FC, (j + 1) * FC), slice(D_FF + j * FC, D_FF + (j + 1) * FC)

    def up_proj(j):
        for part, cs in enumerate(cols(j)):
            _stage_rows(work_ref, j % slots, jnp.dot(h, wup_ref[:, cs].astype(BF16), preferred_element_type=F32),
                        part * FC // LANES)

    def activation(j):
        cw = jnp.concatenate([cw_ref[:, cs] for cs in cols(j)], axis=1)
        c = _causal_conv3_staged(work_ref, j % slots, carry_ref, j, cw)
        a, u = c[:, :FC], c[:, FC:]
        return (a * jax.nn.sigmoid(a) * u).astype(BF16)

    def down_proj(j, act):
        base = x_ref if j == 0 else o_ref
        w = wdn_ref[j * FC:(j + 1) * FC, :].astype(BF16)
        if j == n_chunks - 1:
            for rows in (slice(0, TM // 2), slice(TM // 2, TM)):
                out = base[rows, :] + jnp.dot(act[rows], w, preferred_element_type=F32)
                o_ref[rows, :] = _rmsnorm(out, gf_ref[...]) if final_norm else out
        else:
            o_ref[...] = base[...] + jnp.dot(act, w, preferred_element_type=F32)

    for j in range(slots - 1):
        up_proj(j)
    prev = None
    for j in range(n_chunks):
        if j + slots - 1 < n_chunks:
            up_proj(j + slots - 1)
        act = activation(j)
        if prev is not None:
            down_proj(j - 1, prev)
        prev = act
    down_proj(n_chunks - 1, prev)


def _conv_ffn(x, g, w_up, conv_w_t, w_down, layer, g_final, final_norm, seq):
    m = x.shape[0]
    nc = D_FF // FC
    kern = functools.partial(_ffn_kernel, tiles_per_seq=seq // TM, n_chunks=nc, final_norm=final_norm, layer=layer)
    return pl.pallas_call(
        kern,
        out_shape=jax.ShapeDtypeStruct((m, D_MODEL), F32),
        grid=(m // TM,),
        in_specs=[
            pl.BlockSpec((TM, D_MODEL), lambda i: (i, 0)),
            _resident(g.shape),
            _resident_layer((D_MODEL, 2 * D_FF), layer),
            _resident_layer((3, 2 * D_FF), layer),
            _resident_layer((D_FF, D_MODEL), layer),
            _resident((1, D_MODEL)),
        ],
        out_specs=pl.BlockSpec((TM, D_MODEL), lambda i: (i, 0)),
        scratch_shapes=[
            pltpu.VMEM((nc, 2 * FC // LANES, SUBLANES, LANES), F32),
            pltpu.VMEM((FFN_AHEAD + 1, 2 * FC // LANES, SUBLANES + TM, LANES), F32),
        ],
        compiler_params=_params("arbitrary"),
        name="conv_ffn",
    )(x, g, w_up, conv_w_t, w_down, g_final.reshape(1, D_MODEL))


def _sc_kernel(x_ref, g_ref, win_ref, cw_ref, wout_ref, o_ref, carry_ref, work_ref, b_ref, *,
               tiles_per_seq, n_chunks, g_row):
    slots = work_ref.shape[0]
    h = _rmsnorm(x_ref[...], g_ref[g_row:g_row + 1, :]).astype(BF16)

    @pl.when(pl.program_id(0) % tiles_per_seq == 0)
    def _():
        carry_ref[...] = jnp.zeros_like(carry_ref)

    def in_proj(j):
        b, c, u = (jnp.dot(h, win_ref[:, part * SC_WIDTH + j * FC:part * SC_WIDTH + (j + 1) * FC].astype(BF16),
                           preferred_element_type=F32) for part in range(3))
        b_ref[j % slots] = b
        _stage_rows(work_ref, j % slots, c * u)

    def out_proj(j):
        conv = _causal_conv3_staged(work_ref, j % slots, carry_ref, j, cw_ref[:, j * FC:(j + 1) * FC])
        y = (b_ref[j % slots] * conv).astype(BF16)
        base = x_ref if j == 0 else o_ref
        o_ref[...] = base[...] + jnp.dot(y, wout_ref[j * FC:(j + 1) * FC, :].astype(BF16),
                                         preferred_element_type=F32)

    for j in range(slots - 1):
        in_proj(j)
    for j in range(n_chunks):
        if j + slots - 1 < n_chunks:
            in_proj(j + slots - 1)
        out_proj(j)


def _short_conv(x, g, g_row, w_in, conv_w_t, w_out, seq):
    m = x.shape[0]
    nc = SC_WIDTH // FC
    kern = functools.partial(_sc_kernel, tiles_per_seq=seq // SC_TM, n_chunks=nc, g_row=g_row)
    return pl.pallas_call(
        kern,
        out_shape=jax.ShapeDtypeStruct((m, D_MODEL), F32),
        grid=(m // SC_TM,),
        in_specs=[
            pl.BlockSpec((SC_TM, D_MODEL), lambda i: (i, 0)),
            _resident(g.shape),
            _resident((D_MODEL, 3 * SC_WIDTH)),
            _resident((3, SC_WIDTH)),
            _resident((SC_WIDTH, D_MODEL)),
        ],
        out_specs=pl.BlockSpec((SC_TM, D_MODEL), lambda i: (i, 0)),
        scratch_shapes=[
            pltpu.VMEM((nc, FC // LANES, SUBLANES, LANES), F32),
            pltpu.VMEM((SC_AHEAD + 1, FC // LANES, SUBLANES + SC_TM, LANES), F32),
            pltpu.VMEM((SC_AHEAD + 1, SC_TM, FC), F32),
        ],
        compiler_params=_params("arbitrary"),
        name="short_conv",
    )(x, g, w_in, conv_w_t, w_out)


def _chunk_cumsum(x):
    rows, w = x.shape
    per = GLA_CHUNK // SUBLANES
    g = x.reshape(rows // SUBLANES, SUBLANES, w)
    sub = lax.broadcasted_iota(jnp.int32, g.shape, 1)
    shift = 1
    while shift < SUBLANES:
        g = g + jnp.where(sub >= shift, pltpu.roll(g, shift, 1), 0.0)
        shift *= 2
    g = g.reshape(rows // GLA_CHUNK, per, SUBLANES, w)
    tot = jnp.broadcast_to(g[:, :, SUBLANES - 1:SUBLANES, :], g.shape)
    parts = [g[:, 0]]
    off = tot[:, 0]
    for i in range(1, per):
        parts.append(g[:, i] + off)
        off = off + tot[:, i]
    return jnp.stack(parts, axis=1).reshape(rows, w)


def _gla_kernel(x_ref, g_ref, wint_ref, wgu_ref, bg_ref, hg_ref, wout_ref, o_ref,
                st_ref, q_ref, k_ref, qin_ref, kin_ref, kdec_ref, v_ref, gate_ref, y_ref, h_ref, z_ref, dec_ref,
                winb_ref, wab_ref, wgub_ref, woutb_ref, *, g_row):
    n_chunks = SB // GLA_CHUNK
    nt = (((1,), (1,)), ((), ()))
    tn = (((0,), (0,)), ((), ()))
    heads = range(GLA_HEADS)
    ks = [slice(hd * GLA_HEAD_K, (hd + 1) * GLA_HEAD_K) for hd in heads]
    vs = [slice(hd * GLA_HEAD_V, (hd + 1) * GLA_HEAD_V) for hd in heads]
    rows = [slice(n * GLA_CHUNK, (n + 1) * GLA_CHUNK) for n in range(n_chunks)]

    @pl.when(pl.program_id(1) == 0)
    def _():
        st_ref[...] = jnp.zeros_like(st_ref)

    @pl.when((pl.program_id(0) == 0) & (pl.program_id(1) == 0))
    def _():
        for c0 in range(0, winb_ref.shape[1], LANES):
            winb_ref[:, c0:c0 + LANES] = wint_ref[c0:c0 + LANES, :].T.astype(BF16)
        wa_t = jnp.concatenate([wint_ref[winb_ref.shape[1]:, :],
                                jnp.zeros((LANES - GLA_GATE_RANK, D_MODEL), F32)], axis=0)
        wab_ref[...] = wa_t.T.astype(BF16)
        wgub_ref[...] = jnp.concatenate([wgu_ref[...], jnp.zeros((LANES - GLA_GATE_RANK, GLA_KEY_DIM), F32)],
                                        axis=0).astype(BF16)
        woutb_ref[...] = wout_ref[...].astype(BF16)

    r = lax.broadcasted_iota(jnp.int32, (SB, SB), 0)
    c = lax.broadcasted_iota(jnp.int32, (SB, SB), 1)
    keep = (c <= r) & (c >= r - r % GLA_CHUNK)


    def front_a(sb):
        s = sb % 2
        h = _rmsnorm(x_ref[sb * SB:(sb + 1) * SB, :], g_ref[g_row:g_row + 1, :]).astype(BF16)
        h_ref[s] = h
        a_low = jnp.dot(h, wab_ref[...], preferred_element_type=F32).astype(BF16)
        q_ref[s] = jnp.dot(h, winb_ref[:, :GLA_KEY_DIM], preferred_element_type=F32) * (GLA_HEAD_K ** -0.5)
        k_ref[s] = jnp.dot(h, winb_ref[:, GLA_KEY_DIM:2 * GLA_KEY_DIM], preferred_element_type=F32)
        z_ref[s] = jnp.dot(a_low, wgub_ref[...], preferred_element_type=F32) + bg_ref[...]

    def front_b(sb):
        s = sb % 2
        h = h_ref[s]
        v_ref[s] = jnp.dot(h, winb_ref[:, 2 * GLA_KEY_DIM:2 * GLA_KEY_DIM + GLA_VAL_DIM],
                           preferred_element_type=F32).astype(BF16)
        gate_ref[s] = jnp.dot(h, winb_ref[:, 2 * GLA_KEY_DIM + GLA_VAL_DIM:], preferred_element_type=F32)
        z = z_ref[s]
        log_a = (jnp.minimum(z, 0.0) - jnp.log(1.0 + jnp.exp(-jnp.abs(z)))) * (1.0 / GLA_GATE_TAU)
        bcum = _chunk_cumsum(log_a).reshape(n_chunks, GLA_CHUNK, GLA_KEY_DIM)
        b_last = bcum[:, GLA_CHUNK - 1:, :]
        dec_ref[s] = jnp.broadcast_to(jnp.exp(b_last), dec_ref.shape[1:])
        to_rows = lambda t: t.reshape(SB, GLA_KEY_DIM)
        k_in = k_ref[s] * to_rows(jnp.exp(-bcum))
        kin_ref[s] = k_in.astype(BF16)
        decay_rows = jnp.broadcast_to(jnp.exp(b_last), bcum.shape)
        kdec_ref[s] = (k_in * to_rows(decay_rows)).astype(BF16)
        qin_ref[s] = (q_ref[s] * to_rows(jnp.exp(bcum))).astype(BF16)

    def back_a(sb):
        s = sb % 2
        for grp in (heads[:GLA_HEADS // 2], heads[GLA_HEADS // 2:]):
            kv_t = {hd: [lax.dot_general(v_ref[s, rows[n], vs[hd]], kdec_ref[s, rows[n], ks[hd]], tn,
                                         preferred_element_type=F32) for n in range(n_chunks)] for hd in grp}
            scores = {hd: lax.dot_general(qin_ref[s, :, ks[hd]], kin_ref[s, :, ks[hd]], nt,
                                          preferred_element_type=F32) for hd in grp}
            probs = {hd: jnp.where(keep, scores[hd], 0.0).astype(BF16) for hd in grp}
            o_intra = {hd: jnp.dot(probs[hd], v_ref[s, :, vs[hd]], preferred_element_type=F32) for hd in grp}
            st = {hd: st_ref[hd] for hd in grp}
            o_inter = {hd: [None] * n_chunks for hd in grp}
            for n in range(n_chunks):
                for hd in grp:
                    o_inter[hd][n] = lax.dot_general(qin_ref[s, rows[n], ks[hd]], st[hd].astype(BF16), nt,
                                                     preferred_element_type=F32)
                for hd in grp:
                    st[hd] = dec_ref[s, n, 0:1, ks[hd]] * st[hd] + kv_t[hd][n]
            for hd in grp:
                st_ref[hd] = st[hd]
            for hd in grp:
                o = o_intra[hd] + jnp.concatenate(o_inter[hd], axis=0)
                on = o * lax.rsqrt(jnp.mean(o * o, axis=-1, keepdims=True) + RMS_EPS) * hg_ref[hd:hd + 1, :]
                gate = gate_ref[s, :, vs[hd]]
                y_ref[s, :, vs[hd]] = (on * (gate * jax.nn.sigmoid(gate))).astype(BF16)

    def back_b(sb):
        blk = slice(sb * SB, (sb + 1) * SB)
        o_ref[blk, :] = x_ref[blk, :] + jnp.dot(y_ref[sb % 2], woutb_ref[...], preferred_element_type=F32)

    n_sub = TB // SB
    front_a(0)
    front_b(0)
    for sb in range(n_sub):
        if sb + 1 < n_sub:
            front_a(sb + 1)
        back_a(sb)
        if sb + 1 < n_sub:
            front_b(sb + 1)
        back_b(sb)


def _gla_mixer(x, g, g_row, w_in, w_gate_up, b_gate, head_g, w_out, batch):
    m = x.shape[0]
    tiles = m // batch // TB
    tile = pl.BlockSpec((TB, D_MODEL), lambda b, t: (b * tiles + t, 0))
    return pl.pallas_call(
        functools.partial(_gla_kernel, g_row=g_row),
        out_shape=jax.ShapeDtypeStruct((m, D_MODEL), F32),
        grid=(batch, tiles),
        in_specs=[
            tile,
            _resident(g.shape),
            _resident(w_in.T.shape),
            _resident((GLA_GATE_RANK, GLA_KEY_DIM)),
            _resident((1, GLA_KEY_DIM)),
            _resident((GLA_HEADS, GLA_HEAD_V)),
            _resident((GLA_VAL_DIM, D_MODEL)),
        ],
        out_specs=tile,
        scratch_shapes=[
            pltpu.VMEM((GLA_HEADS, GLA_HEAD_V, GLA_HEAD_K), F32),
            pltpu.VMEM((2, SB, GLA_KEY_DIM), F32),
            pltpu.VMEM((2, SB, GLA_KEY_DIM), F32),
            pltpu.VMEM((2, SB, GLA_KEY_DIM), BF16),
            pltpu.VMEM((2, SB, GLA_KEY_DIM), BF16),
            pltpu.VMEM((2, SB, GLA_KEY_DIM), BF16),
            pltpu.VMEM((2, SB, GLA_VAL_DIM), BF16),
            pltpu.VMEM((2, SB, GLA_VAL_DIM), F32),
            pltpu.VMEM((2, SB, GLA_VAL_DIM), BF16),
            pltpu.VMEM((2, SB, D_MODEL), BF16),
            pltpu.VMEM((2, SB, GLA_KEY_DIM), F32),
            pltpu.VMEM((2, SB // GLA_CHUNK, SUBLANES, GLA_KEY_DIM), F32),
            pltpu.VMEM((D_MODEL, 2 * GLA_KEY_DIM + 2 * GLA_VAL_DIM), BF16),
            pltpu.VMEM((D_MODEL, LANES), BF16),
            pltpu.VMEM((LANES, GLA_KEY_DIM), BF16),
            pltpu.VMEM((GLA_VAL_DIM, D_MODEL), BF16),
        ],
        compiler_params=_params("arbitrary", "arbitrary"),
        name="gla_mixer",
    )(x, g, w_in.T, w_gate_up, b_gate, head_g, w_out)


def kernel(x, norm_mix_g, norm_ffn_g, gla_w_in, gla_w_gate_up, gla_b_gate, gla_head_norm_g, gla_w_out,
           sc_w_in, sc_conv_w, sc_w_out, ffn_w_up, ffn_conv_w, ffn_w_down, final_norm_g):
    batch, seq, d = x.shape
    assert d == D_MODEL and seq % TM == 0 and seq % SC_TM == 0 and seq % TB == 0
    xf = x.reshape(batch * seq, d)

    ffn_conv_w_t = jnp.swapaxes(ffn_conv_w, 1, 2)
    xf = _gla_mixer(xf, norm_mix_g, 0, gla_w_in[0], gla_w_gate_up[0], gla_b_gate, gla_head_norm_g[0],
                    gla_w_out[0], batch)
    xf = _conv_ffn(xf, norm_ffn_g, ffn_w_up, ffn_conv_w_t, ffn_w_down, 0, final_norm_g, False, seq)
    xf = _short_conv(xf, norm_mix_g, 1, sc_w_in[0], sc_conv_w[0].T, sc_w_out[0], seq)
    xf = _conv_ffn(xf, norm_ffn_g, ffn_w_up, ffn_conv_w_t, ffn_w_down, 1, final_norm_g, True, seq)
    return xf.reshape(batch, seq, d)
```
